```python
import jax, jax.numpy as jnp
from jax import lax
import numpy as np

D_MODEL = 2048
BATCH = 2
SEQ = 8192
DEPTH = 2

GRID_W = 64
CTX_LEN = 256
ROPE_THETA = 10000.0
NORM_EPS = 1e-6
Q_BLOCK = 128

GQA_HEADS = 8
GQA_KV_HEADS = 2
HEAD_DIM = 128
GLA_HEADS = 4
GLA_DK = 128
GLA_DV = 256
GLA_GATE_RANK = 16
GLA_TAU = 16.0
GLA_CHUNK = 64
MLA_HEADS = 8
MLA_Q_RANK = 512
MLA_KV_RANK = 512
MLA_NOPE = 128
MLA_ROPE = 64
MLA_V = 128
GQA_W = GQA_HEADS * HEAD_DIM
GLA_W = GLA_HEADS * GLA_DV
MLA_W = MLA_HEADS * MLA_V
PEER_HEADS = 8
PEER_NKEYS = 128
PEER_EXPERTS = PEER_NKEYS * PEER_NKEYS
PEER_QDIM = 256
PEER_HALF = PEER_QDIM // 2
PEER_TOPK = 16
PEER_BLOCK = 128

IN_SPLITS = (GQA_HEADS * HEAD_DIM, GQA_KV_HEADS * HEAD_DIM, GQA_KV_HEADS * HEAD_DIM,
             GLA_HEADS * GLA_DK, GLA_HEADS * GLA_DK, GLA_HEADS * GLA_DV, GLA_HEADS * GLA_DV,
             GLA_GATE_RANK, GLA_GATE_RANK,
             MLA_Q_RANK, MLA_KV_RANK, MLA_ROPE,
             D_MODEL, D_MODEL, D_MODEL)
IN_COLS = sum(IN_SPLITS)

kernel_name = 'hybrid_gqa_gla_mla_peer_dit'


def rms_norm(x, g):
    xf = x.astype(jnp.float32)
    y = xf * lax.rsqrt(jnp.mean(xf * xf, axis=-1, keepdims=True) + NORM_EPS)
    return (y * g.astype(jnp.float32)).astype(x.dtype)


def split_cols(p, sizes):
    out, off = [], 0
    for n in sizes:
        out.append(p[..., off:off + n])
        off += n
    return out


def axial_rope(row_idx, col_idx, dim):
    quarter = dim // 4
    inv_freq = ROPE_THETA ** (-jnp.arange(quarter, dtype=jnp.float32) / quarter)
    ang = jnp.concatenate([row_idx[:, None].astype(jnp.float32) * inv_freq,
                           col_idx[:, None].astype(jnp.float32) * inv_freq], axis=-1)
    return jnp.cos(ang), jnp.sin(ang)


def apply_rope(x, cos, sin):
    half = x.shape[-1] // 2
    xf = x.astype(jnp.float32)
    x1, x2 = xf[..., :half], xf[..., half:]
    cs, sn = cos[None, :, None, :], sin[None, :, None, :]
    return jnp.concatenate([x1 * cs - x2 * sn, x2 * cs + x1 * sn], axis=-1).astype(x.dtype)


def block_attention(q, k, v):
    B, Sq, Hkv, G, dq = q.shape
    dv = v.shape[-1]
    nblk = Sq // Q_BLOCK
    scale = dq ** -0.5
    qb = q.reshape(B, nblk, Q_BLOCK, Hkv, G, dq).transpose(1, 0, 2, 3, 4, 5)

    def one_block(qblk):
        s = jnp.einsum('bqhgd,bkhd->bhgqk', qblk, k, preferred_element_type=jnp.float32) * scale
        p = jax.nn.softmax(s, axis=-1)
        return jnp.einsum('bhgqk,bkhe->bqhge', p.astype(v.dtype), v)

    o = lax.map(one_block, qb)
    return o.transpose(1, 0, 2, 3, 4, 5).reshape(B, Sq, Hkv, G, dv)


def gla_scan(q, k, v, log_a, state0):
    B, S, H, dk = q.shape
    dv = v.shape[-1]
    n = S // GLA_CHUNK

    def chunks(t):
        return t.reshape(B, n, GLA_CHUNK, H, t.shape[-1]).transpose(1, 0, 3, 2, 4).astype(jnp.float32)

    tri = jnp.tril(jnp.ones((GLA_CHUNK, GLA_CHUNK), dtype=bool))[:, :, None]

    def step(state, inp):
        qc, kc, vc, ac = inp
        b = jnp.cumsum(ac, axis=2)
        b_end = b[:, :, -1:, :]
        o_inter = jnp.einsum('bhtd,bhde->bhte', qc * jnp.exp(b), state)
        decay = jnp.exp(jnp.where(tri, b[:, :, :, None, :] - b[:, :, None, :, :], -jnp.inf))
        scores = jnp.einsum('bhtd,bhsd,bhtsd->bhts', qc, kc, decay)
        o_intra = jnp.einsum('bhts,bhse->bhte', scores, vc)
        new_state = (jnp.exp(b_end[:, :, 0, :, None]) * state
                     + jnp.einsum('bhsd,bhse->bhde', kc * jnp.exp(b_end - b), vc))
        return new_state, o_inter + o_intra

    state, o = lax.scan(step, state0, (chunks(q), chunks(k), chunks(v), chunks(log_a)))
    o = o.transpose(1, 0, 3, 2, 4).reshape(B, S, H, dv).astype(v.dtype)
    return o, state


def gla_scan_reverse(q, k, v, log_a, state0):
    o, state = gla_scan(jnp.flip(q, 1), jnp.flip(k, 1), jnp.flip(v, 1), jnp.flip(log_a, 1), state0)
    return jnp.flip(o, 1), state


def stream_inputs(p, lp, rope_a, rope_m):
    B, S = p.shape[:2]
    aq, ak, av, lq, lk, lv, lg, lrf, lrb, mcq, mckv, mkr, ga, gl, gm = split_cols(p, IN_SPLITS)
    aq = rms_norm(aq.reshape(B, S, GQA_HEADS, HEAD_DIM), lp['gqa_qn_g'])
    ak = rms_norm(ak.reshape(B, S, GQA_KV_HEADS, HEAD_DIM), lp['gqa_kn_g'])
    mq = (rms_norm(mcq, lp['mla_qn_g']) @ lp['mla_wuq']).reshape(B, S, MLA_HEADS, MLA_NOPE + MLA_ROPE)
    mkv = (rms_norm(mckv, lp['mla_kvn_g']) @ lp['mla_wukv']).reshape(B, S, MLA_HEADS, MLA_NOPE + MLA_V)
    mq_nope, mq_rope = mq[..., :MLA_NOPE], mq[..., MLA_NOPE:]
    mk_nope, mv = mkv[..., :MLA_NOPE], mkv[..., MLA_NOPE:]
    mk_rope = mkr.reshape(B, S, 1, MLA_ROPE)
    if rope_a is not None:
        aq = apply_rope(aq, *rope_a)
        ak = apply_rope(ak, *rope_a)
        mq_rope = apply_rope(mq_rope, *rope_m)
        mk_rope = apply_rope(mk_rope, *rope_m)

    def log_decay(lr, w, bias):
        pre = (lr @ w + bias).astype(jnp.float32)
        return (jax.nn.log_sigmoid(pre) / GLA_TAU).reshape(B, S, GLA_HEADS, GLA_DK)

    return dict(
        a_q=aq.reshape(B, S, GQA_KV_HEADS, GQA_HEADS // GQA_KV_HEADS, HEAD_DIM),
        a_k=ak,
        a_v=av.reshape(B, S, GQA_KV_HEADS, HEAD_DIM),
        l_q=lq.reshape(B, S, GLA_HEADS, GLA_DK) * GLA_DK ** -0.5,
        l_k=lk.reshape(B, S, GLA_HEADS, GLA_DK),
        l_v=lv.reshape(B, S, GLA_HEADS, GLA_DV),
        l_g=lg,
        l_af=log_decay(lrf, lp['gla_wa2_f'], lp['gla_ba_f']),
        l_ab=log_decay(lrb, lp['gla_wa2_b'], lp['gla_ba_b']),
        m_q=jnp.concatenate([mq_nope, mq_rope], axis=-1)[:, :, :, None, :],
        m_k=jnp.concatenate([mk_nope, jnp.broadcast_to(mk_rope, (B, S, MLA_HEADS, MLA_ROPE))], axis=-1),
        m_v=mv,
        gates=(ga, gl, gm),
    )


def gla_readout(o, gate, g):
    B, S = o.shape[:2]
    y = rms_norm(o, g) * jax.nn.silu(gate.reshape(B, S, GLA_HEADS, GLA_DV))
    return y.reshape(B, S, GLA_W)


def merge_branches(o_a, o_l, o_m, gates, lp):
    ga, gl, gm = gates
    y = (jax.nn.sigmoid(ga) * (o_a @ lp['w_br_gqa'])
         + jax.nn.sigmoid(gl) * (o_l @ lp['w_br_gla'])
         + jax.nn.sigmoid(gm) * (o_m @ lp['w_br_mla']))
    return y @ lp['w_out']


def token_mixer(h_l, h_c, lp, rope_a, rope_m, need_ctx):
    B, S = h_l.shape[:2]
    Sc = h_c.shape[1]
    L = stream_inputs(h_l @ lp['w_in'], lp, rope_a, rope_m)
    C = stream_inputs(h_c @ lp['w_in'], lp, None, None)
    o_a = block_attention(L['a_q'], jnp.concatenate([C['a_k'], L['a_k']], axis=1),
                          jnp.concatenate([C['a_v'], L['a_v']], axis=1)).reshape(B, S, GQA_W)
    o_m = block_attention(L['m_q'], jnp.concatenate([C['m_k'], L['m_k']], axis=1),
                          jnp.concatenate([C['m_v'], L['m_v']], axis=1)).reshape(B, S, MLA_W)
    zero = jnp.zeros((B, GLA_HEADS, GLA_DK, GLA_DV), jnp.float32)
    oc_f, st_f = gla_scan(C['l_q'], C['l_k'], C['l_v'], C['l_af'], zero)
    oc_b, st_b = gla_scan_reverse(C['l_q'], C['l_k'], C['l_v'], C['l_ab'], zero)
    ol_f, _ = gla_scan(L['l_q'], L['l_k'], L['l_v'], L['l_af'], st_f)
    ol_b, _ = gla_scan_reverse(L['l_q'], L['l_k'], L['l_v'], L['l_ab'], st_b)
    o_l = gla_readout(ol_f + ol_b, L['l_g'], lp['gla_on_g'])
    y_l = merge_branches(o_a, o_l, o_m, L['gates'], lp)
    if not need_ctx:
        return y_l, None
    oc_a = block_attention(C['a_q'], C['a_k'], C['a_v']).reshape(B, Sc, GQA_W)
    oc_m = block_attention(C['m_q'], C['m_k'], C['m_v']).reshape(B, Sc, MLA_W)
    oc_l = gla_readout(oc_f + oc_b, C['l_g'], lp['gla_on_g'])
    y_c = merge_branches(oc_a, oc_l, oc_m, C['gates'], lp)
    return y_l, y_c


def peer_ffn(h, wq, k1, k2, u_tab, v_tab):
    T, D = h.shape
    n = T // PEER_BLOCK

    def one_block(hb):
        q = (hb @ wq).reshape(PEER_BLOCK, PEER_HEADS, 2, PEER_HALF)
        s1 = jnp.einsum('thd,hnd->thn', q[:, :, 0], k1, preferred_element_type=jnp.float32)
        s2 = jnp.einsum('thd,hnd->thn', q[:, :, 1], k2, preferred_element_type=jnp.float32)
        v1, i1 = lax.top_k(s1, PEER_TOPK)
        v2, i2 = lax.top_k(s2, PEER_TOPK)
        ncand = PEER_TOPK * PEER_TOPK
        cand = (v1[..., :, None] + v2[..., None, :]).reshape(PEER_BLOCK, PEER_HEADS, ncand)
        cand_idx = (i1[..., :, None] * PEER_NKEYS + i2[..., None, :]).reshape(PEER_BLOCK, PEER_HEADS, ncand)
        best, pos = lax.top_k(cand, PEER_TOPK)
        idx = jnp.take_along_axis(cand_idx, pos, axis=-1)
        g = jax.nn.softmax(best, axis=-1)
        act = jax.nn.gelu(jnp.einsum('td,thkd->thk', hb, u_tab[idx]), approximate=False)
        return jnp.einsum('thk,thkd->td', (g * act).astype(hb.dtype), v_tab[idx])

    return lax.map(one_block, h.reshape(n, PEER_BLOCK, D)).reshape(T, D)


def ada_params(cvec, w, b):
    return jnp.split(jax.nn.silu(cvec) @ w + b, 6, axis=-1)


def setup_inputs(seed: int = 0) -> dict:
    key = jax.random.key(seed)
    ks = jax.random.split(key, 40)
    counter = iter(range(40))
    L, D = DEPTH, D_MODEL

    def normal(shape, scale):
        return jax.random.normal(ks[next(counter)], shape, jnp.float32) * scale

    def gain(shape):
        return 1.0 + normal(shape, 0.02)

    return {
        'x': normal((BATCH, SEQ, D), 1.0),
        'c': normal((BATCH, D), 1.0),
        'ctx': normal((BATCH, CTX_LEN, D), 1.0),
        'c_ctx': normal((D,), 1.0),
        'w_mod': normal((L, D, 6 * D), 0.5 * D ** -0.5),
        'b_mod': normal((L, 6 * D), 0.01),
        'norm1_g': gain((L, D)),
        'w_in': normal((L, D, IN_COLS), D ** -0.5),
        'gqa_qn_g': gain((L, HEAD_DIM)),
        'gqa_kn_g': gain((L, HEAD_DIM)),
        'gla_wa2_f': normal((L, GLA_GATE_RANK, GLA_HEADS * GLA_DK), GLA_GATE_RANK ** -0.5),
        'gla_ba_f': normal((L, GLA_HEADS * GLA_DK), 0.1),
        'gla_wa2_b': normal((L, GLA_GATE_RANK, GLA_HEADS * GLA_DK), GLA_GATE_RANK ** -0.5),
        'gla_ba_b': normal((L, GLA_HEADS * GLA_DK), 0.1),
        'gla_on_g': gain((L, GLA_DV)),
        'mla_qn_g': gain((L, MLA_Q_RANK)),
        'mla_wuq': normal((L, MLA_Q_RANK, MLA_HEADS * (MLA_NOPE + MLA_ROPE)), MLA_Q_RANK ** -0.5),
        'mla_kvn_g': gain((L, MLA_KV_RANK)),
        'mla_wukv': normal((L, MLA_KV_RANK, MLA_HEADS * (MLA_NOPE + MLA_V)), MLA_KV_RANK ** -0.5),
        'w_br_gqa': normal((L, GQA_W, D), GQA_W ** -0.5),
        'w_br_gla': normal((L, GLA_W, D), GLA_W ** -0.5),
        'w_br_mla': normal((L, MLA_W, D), MLA_W ** -0.5),
        'w_out': normal((L, D, D), D ** -0.5),
        'norm2_g': gain((L, D)),
        'peer_wq': normal((L, D, PEER_HEADS * PEER_QDIM), D ** -0.5),
        'peer_k1': normal((L, PEER_HEADS, PEER_NKEYS, PEER_HALF), PEER_HALF ** -0.5),
        'peer_k2': normal((L, PEER_HEADS, PEER_NKEYS, PEER_HALF), PEER_HALF ** -0.5),
        'peer_u': normal((L, PEER_EXPERTS, D), D ** -0.5),
        'peer_v': normal((L, PEER_EXPERTS, D), 0.5),
        'final_g': gain((D,)),
    }


def reference(x, c, ctx, c_ctx, w_mod, b_mod, norm1_g, w_in, gqa_qn_g, gqa_kn_g,
              gla_wa2_f, gla_ba_f, gla_wa2_b, gla_ba_b, gla_on_g,
              mla_qn_g, mla_wuq, mla_kvn_g, mla_wukv,
              w_br_gqa, w_br_gla, w_br_mla, w_out, norm2_g,
              peer_wq, peer_k1, peer_k2, peer_u, peer_v, final_g):
    B, S, D = x.shape
    rows = S // GRID_W
    row_idx = jnp.repeat(jnp.arange(rows, dtype=jnp.int32), GRID_W)
    col_idx = jnp.tile(jnp.arange(GRID_W, dtype=jnp.int32), rows)
    rope_a = axial_rope(row_idx, col_idx, HEAD_DIM)
    rope_m = axial_rope(row_idx, col_idx, MLA_ROPE)
    for i in range(DEPTH):
        need_ctx = i < DEPTH - 1
        lp = dict(w_in=w_in[i], gqa_qn_g=gqa_qn_g[i], gqa_kn_g=gqa_kn_g[i],
                  gla_wa2_f=gla_wa2_f[i], gla_ba_f=gla_ba_f[i], gla_wa2_b=gla_wa2_b[i], gla_ba_b=gla_ba_b[i],
                  gla_on_g=gla_on_g[i], mla_qn_g=mla_qn_g[i], mla_wuq=mla_wuq[i],
                  mla_kvn_g=mla_kvn_g[i], mla_wukv=mla_wukv[i],
                  w_br_gqa=w_br_gqa[i], w_br_gla=w_br_gla[i], w_br_mla=w_br_mla[i], w_out=w_out[i])
        sh1, sc1, g1, sh2, sc2, g2 = [m[:, None, :] for m in ada_params(c, w_mod[i], b_mod[i])]
        mc = ada_params(c_ctx, w_mod[i], b_mod[i])
        h_l = rms_norm(x, norm1_g[i]) * (1.0 + sc1) + sh1
        h_c = rms_norm(ctx, norm1_g[i]) * (1.0 + mc[1]) + mc[0]
        y_l, y_c = token_mixer(h_l, h_c, lp, rope_a, rope_m, need_ctx)
        x = x + g1 * y_l
        h2 = rms_norm(x, norm2_g[i]) * (1.0 + sc2) + sh2
        x = x + g2 * peer_ffn(h2.reshape(B * S, D), peer_wq[i], peer_k1[i], peer_k2[i],
                              peer_u[i], peer_v[i]).reshape(B, S, D)
        if need_ctx:
            ctx = ctx + mc[2] * y_c
            h2c = rms_norm(ctx, norm2_g[i]) * (1.0 + mc[4]) + mc[3]
            ctx = ctx + mc[5] * peer_ffn(h2c.reshape(-1, D), peer_wq[i], peer_k1[i], peer_k2[i],
                                         peer_u[i], peer_v[i]).reshape(ctx.shape)
    return rms_norm(x, final_g)
```

```python
import functools

import jax
import jax.numpy as jnp
from jax import lax
from jax.experimental import pallas as pl
from jax.experimental.pallas import tpu as pltpu

f32 = jnp.float32
bf16 = jnp.bfloat16

D_MODEL = 2048
GRID_W = 64
ROPE_THETA = 10000.0
NORM_EPS = 1e-6

GQA_HEADS = 8
GQA_KV_HEADS = 2
HEAD_DIM = 128
GLA_HEADS = 4
GLA_DK = 128
GLA_DV = 256
GLA_GATE_RANK = 16
GLA_TAU = 16.0
GLA_CHUNK = 64
MLA_HEADS = 8
MLA_Q_RANK = 512
MLA_KV_RANK = 512
MLA_NOPE = 128
MLA_ROPE = 64
MLA_V = 128
MLA_QK_PAD = 256
GQA_W = GQA_HEADS * HEAD_DIM
GLA_W = GLA_HEADS * GLA_DV
MLA_W = MLA_HEADS * MLA_V
PEER_HEADS = 8
PEER_NKEYS = 128
PEER_QDIM = 256
PEER_HALF = PEER_QDIM // 2
PEER_TOPK = 16
PEER_SEL = PEER_HEADS * PEER_TOPK

VMEM_LIMIT = 48 * 1024 * 1024
ROW_TILE = 256


def _params(*sem):
    return pltpu.CompilerParams(dimension_semantics=sem, vmem_limit_bytes=VMEM_LIMIT)


def _mm_kernel(a_ref, b_ref, o_ref):
    o_ref[...] = jnp.dot(a_ref[...].astype(bf16), b_ref[...],
                         preferred_element_type=f32).astype(o_ref.dtype)


def mm(a, b, *, tm=512, tn=None, out_dtype=f32):
    M, K = a.shape
    N = b.shape[1]
    tn = N if tn is None else tn
    tm = min(tm, M)
    assert M % tm == 0 and N % tn == 0, (M, N, tm, tn)
    return pl.pallas_call(
        _mm_kernel,
        grid=(N // tn, M // tm),
        in_specs=[pl.BlockSpec((tm, K), lambda j, i: (i, 0)),
                  pl.BlockSpec((K, tn), lambda j, i: (0, j))],
        out_specs=pl.BlockSpec((tm, tn), lambda j, i: (i, j)),
        out_shape=jax.ShapeDtypeStruct((M, N), out_dtype),
        compiler_params=_params("arbitrary", "arbitrary"),
        name="mm",
    )(a, b.astype(bf16))


def _norm_kernel(x_ref, g_ref, sc_ref, sh_ref, o_ref):
    x = x_ref[0]
    y = x * lax.rsqrt(jnp.mean(x * x, axis=-1, keepdims=True) + NORM_EPS) * g_ref[...]
    o_ref[0] = (y * (1.0 + sc_ref[0, 0]) + sh_ref[0, 0]).astype(o_ref.dtype)


def norm_mod(x, g, sc, sh, *, ctx_tiles, out_dtype):
    B, R, D = x.shape
    kind = lambda r: jnp.where(r < ctx_tiles, 0, 1)
    return pl.pallas_call(
        _norm_kernel,
        grid=(B, R // ROW_TILE),
        in_specs=[pl.BlockSpec((1, ROW_TILE, D), lambda b, r: (b, r, 0)),
                  pl.BlockSpec((1, D), lambda b, r: (0, 0)),
                  pl.BlockSpec((1, 1, 1, D), lambda b, r: (b, kind(r), 0, 0)),
                  pl.BlockSpec((1, 1, 1, D), lambda b, r: (b, kind(r), 0, 0))],
        out_specs=pl.BlockSpec((1, ROW_TILE, D), lambda b, r: (b, r, 0)),
        out_shape=jax.ShapeDtypeStruct((B, R, D), out_dtype),
        compiler_params=_params("arbitrary", "arbitrary"),
        name="norm_mod",
    )(x, g.reshape(1, D), sc, sh)


def _attn_kernel(q_ref, k_ref, v_ref, o_ref, *, tk, nk, scale):
    q = q_ref[0]
    tq = q.shape[0]
    dv = v_ref.shape[-1]

    def body(c, carry):
        m, l, acc = carry
        off = pl.multiple_of(c * tk, tk)
        kc = k_ref[0, pl.ds(off, tk), :]
        vc = v_ref[0, pl.ds(off, tk), :]
        s = lax.dot_general(q, kc, (((1,), (1,)), ((), ())), preferred_element_type=f32) * scale
        m_new = jnp.maximum(m, jnp.max(s, axis=1, keepdims=True))
        p = jnp.exp(s - m_new)
        alpha = jnp.exp(m - m_new)
        l = alpha * l + jnp.sum(p, axis=1, keepdims=True)
        acc = alpha * acc + jnp.dot(p.astype(bf16), vc, preferred_element_type=f32)
        return m_new, l, acc

    init = (jnp.full((tq, 1), -jnp.inf, f32), jnp.zeros((tq, 1), f32), jnp.zeros((tq, dv), f32))
    _, l, acc = lax.fori_loop(0, nk, body, init)
    o_ref[0] = (acc / l).astype(o_ref.dtype)


def attention(q, k, v, *, heads, kv_heads, dq, dv, scale, tq, tk):
    B, Sq, _ = q.shape
    Sk = k.shape[1]
    group = heads // kv_heads
    assert Sq % tq == 0 and Sk % tk == 0
    return pl.pallas_call(
        functools.partial(_attn_kernel, tk=tk, nk=Sk // tk, scale=scale),
        grid=(B, heads, Sq // tq),
        in_specs=[pl.BlockSpec((1, tq, dq), lambda b, h, i: (b, i, h)),
                  pl.BlockSpec((1, Sk, dq), lambda b, h, i: (b, 0, h // group)),
                  pl.BlockSpec((1, Sk, dv), lambda b, h, i: (b, 0, h // group))],
        out_specs=pl.BlockSpec((1, tq, dv), lambda b, h, i: (b, i, h)),
        out_shape=jax.ShapeDtypeStruct((B, Sq, heads * dv), f32),
        compiler_params=_params("arbitrary", "arbitrary", "arbitrary"),
        name="attention",
    )(q, k, v)


def _exact_dot_01(t01, x):
    h1 = x.astype(bf16)
    r1 = x - h1.astype(f32)
    h2 = r1.astype(bf16)
    h3 = (r1 - h2.astype(f32)).astype(bf16)
    d = lambda h: jnp.dot(t01, h, preferred_element_type=f32)
    return d(h1) + d(h2) + d(h3)


def _gla_kernel(q_ref, k_ref, v_ref, lr_ref, wa_ref, ba_ref, o_ref, state_ref, b_ref, *, ctx_chunks):
    d = pl.program_id(0)
    c = pl.program_id(2)
    C = GLA_CHUNK

    @pl.when(c == 0)
    def _():
        state_ref[...] = jnp.zeros_like(state_ref)

    sign = 1 - 2 * d
    row = lax.broadcasted_iota(jnp.int32, (C, C), 0)
    col = lax.broadcasted_iota(jnp.int32, (C, C), 1)
    tri = ((row - col) * sign >= 0)
    pre = jnp.dot(lr_ref[0, 0].astype(bf16), wa_ref[0], preferred_element_type=f32) + ba_ref[0]
    log_a = (jnp.minimum(pre, 0.0) - jnp.log1p(jnp.exp(-jnp.abs(pre)))) * (1.0 / GLA_TAU)
    b_ref[...] = _exact_dot_01(tri.astype(bf16), log_a)
    lane = lax.broadcasted_iota(jnp.int32, (C, 128), 1)
    srow = lax.broadcasted_iota(jnp.int32, (C, 128), 0)
    valid = ((srow - lane) * sign >= 0) & (lane < C)

    for h in range(GLA_HEADS):
        ks = slice(h * GLA_DK, (h + 1) * GLA_DK)
        vs = slice(h * GLA_DV, (h + 1) * GLA_DV)
        qh = q_ref[0, :, ks] * (GLA_DK ** -0.5)
        kh = k_ref[0, :, ks]
        vh = v_ref[0, :, vs].astype(bf16)
        bh = b_ref[:, ks]
        b_end = jnp.where(d == 0, b_ref[C - 1:C, ks], b_ref[0:1, ks])
        st = state_ref[h]

        o_inter = lax.dot_general((qh * jnp.exp(bh)).astype(bf16), st.astype(bf16),
                                  (((1,), (1,)), ((), ())), preferred_element_type=f32)

        def col_block(sb, scores):
            base = pl.multiple_of(sb * 8, 8)
            b_rows = b_ref[pl.ds(base, 8), ks]
            k_rows = k_ref[0, pl.ds(base, 8), ks]
            for u in range(8):
                s = sb * 8 + u
                x = qh * jnp.exp(jnp.minimum(bh - b_rows[u:u + 1], 0.0)) * k_rows[u:u + 1]
                scores = jnp.where(lane == s, jnp.sum(x, axis=1, keepdims=True), scores)
            return scores

        scores = lax.fori_loop(0, C // 8, col_block, jnp.zeros((C, 128), f32))
        scores = jnp.where(valid, scores, 0.0)
        o_intra = jnp.dot(scores[:, :C].astype(bf16), vh, preferred_element_type=f32)
        o_ref[0, 0, :, vs] = o_inter + o_intra

        k_dec = (kh * jnp.exp(b_end - bh)).astype(bf16)
        state_ref[h] = st * jnp.exp(b_end) + lax.dot_general(
            vh, k_dec, (((0,), (0,)), ((), ())), preferred_element_type=f32)


def gla_scans(q, k, v, lr, wa, ba, *, ctx_len):
    B, R, _ = q.shape
    C = GLA_CHUNK
    n = R // C
    cc = ctx_len // C

    def chunk(d, c):
        back = jnp.where(c < cc, cc - 1 - c, n - 1 + cc - c)
        return jnp.where(d == 0, c, back)

    return pl.pallas_call(
        functools.partial(_gla_kernel, ctx_chunks=cc),
        grid=(2, B, n),
        in_specs=[pl.BlockSpec((1, C, GLA_HEADS * GLA_DK), lambda d, b, c: (b, chunk(d, c), 0)),
                  pl.BlockSpec((1, C, GLA_HEADS * GLA_DK), lambda d, b, c: (b, chunk(d, c), 0)),
                  pl.BlockSpec((1, C, GLA_W), lambda d, b, c: (b, chunk(d, c), 0)),
                  pl.BlockSpec((1, 1, C, GLA_GATE_RANK), lambda d, b, c: (d, b, chunk(d, c), 0)),
                  pl.BlockSpec((1, GLA_GATE_RANK, GLA_HEADS * GLA_DK), lambda d, b, c: (d, 0, 0)),
                  pl.BlockSpec((1, 1, GLA_HEADS * GLA_DK), lambda d, b, c: (d, 0, 0))],
        out_specs=pl.BlockSpec((1, 1, C, GLA_W), lambda d, b, c: (d, b, chunk(d, c), 0)),
        out_shape=jax.ShapeDtypeStruct((2, B, R, GLA_W), f32),
        scratch_shapes=[pltpu.VMEM((GLA_HEADS, GLA_DV, GLA_DK), f32),
                        pltpu.VMEM((C, GLA_HEADS * GLA_DK), f32)],
        compiler_params=_params("arbitrary", "arbitrary", "arbitrary"),
        name="gla_scans",
    )(q, k, v, lr, wa, ba)


def _top16(s, ids):
    big = jnp.float32(1e9)
    vals, picks = [], []
    for _ in range(PEER_TOPK):
        m = jnp.max(s, axis=0, keepdims=True)
        pick = jnp.min(jnp.where(s == m, ids, big), axis=0, keepdims=True)
        vals.append(m)
        picks.append(pick)
        s = jnp.where(ids == pick, -jnp.inf, s)
    return jnp.concatenate(vals, axis=0), jnp.concatenate(picks, axis=0)


def _router_kernel(q_ref, k1_ref, k2_ref, idx_ref, g_ref):
    q = q_ref[...].astype(bf16)
    nt = (((1,), (1,)), ((), ()))
    s1 = lax.dot_general(k1_ref[0], q[:, :PEER_HALF], nt, preferred_element_type=f32)
    s2 = lax.dot_general(k2_ref[0], q[:, PEER_HALF:], nt, preferred_element_type=f32)
    key_id = lax.broadcasted_iota(jnp.int32, s1.shape, 0).astype(f32)
    v1, i1 = _top16(s1, key_id)
    v2, i2 = _top16(s2, key_id)
    cand = jnp.concatenate([v1[a:a + 1] + v2 for a in range(PEER_TOPK)], axis=0)
    cand_e = jnp.concatenate([i1[a:a + 1] * PEER_NKEYS + i2 for a in range(PEER_TOPK)], axis=0)
    pos = lax.broadcasted_iota(jnp.int32, cand.shape, 0).astype(f32)
    best, bpos = _top16(cand, pos)
    experts = [jnp.max(jnp.where(pos == bpos[r:r + 1], cand_e, -1.0), axis=0, keepdims=True)
               for r in range(PEER_TOPK)]
    e = jnp.exp(best - best[0:1])
    g_ref[0] = e / jnp.sum(e, axis=0, keepdims=True)
    idx_ref[0] = jnp.concatenate(experts, axis=0).astype(jnp.int32)


def peer_route(q, k1, k2):
    T = q.shape[0]
    tt = 128
    return pl.pallas_call(
        _router_kernel,
        grid=(T // tt, PEER_HEADS),
        in_specs=[pl.BlockSpec((tt, PEER_QDIM), lambda i, h: (i, h)),
                  pl.BlockSpec((1, PEER_NKEYS, PEER_HALF), lambda i, h: (h, 0, 0)),
                  pl.BlockSpec((1, PEER_NKEYS, PEER_HALF), lambda i, h: (h, 0, 0))],
        out_specs=[pl.BlockSpec((1, PEER_TOPK, tt), lambda i, h: (h, 0, i)),
                   pl.BlockSpec((1, PEER_TOPK, tt), lambda i, h: (h, 0, i))],
        out_shape=[jax.ShapeDtypeStruct((PEER_HEADS, PEER_TOPK, T), jnp.int32),
                   jax.ShapeDtypeStruct((PEER_HEADS, PEER_TOPK, T), f32)],
        compiler_params=_params("arbitrary", "arbitrary"),
        name="peer_route",
    )(q, k1, k2)


PEER_TT = 128


def _unpack_words(w):
    lo = pltpu.bitcast(w << 16, f32)
    hi = pltpu.bitcast(w & jnp.uint32(0xFFFF0000), f32)
    return lo, hi


def _sum_sublanes_8(parts):
    sub = lax.broadcasted_iota(jnp.int32, (8, 128), 0)
    dist = 4
    while len(parts) > 1:
        nxt = []
        half = len(parts) // 2
        for a in range(half):
            lo_rows = parts[a] + pltpu.roll(parts[a], 8 - dist, axis=0)
            hi_rows = parts[a + half] + pltpu.roll(parts[a + half], dist, axis=0)
            nxt.append(jnp.where((sub & dist) == 0, lo_rows, hi_rows))
        parts = nxt
        dist //= 2
    return parts[0]


def _peer_kernel(idx_ref, tab_ref, h_ref, g_ref, o_ref, buf, sem, part_ref, w_ref):
    NS = PEER_SEL

    def issue(t, slot):
        for j in range(NS):
            e = idx_ref[t * NS + j]
            pltpu.make_async_copy(tab_ref.at[e], buf.at[slot, j], sem.at[slot]).start()

    def wait(slot):
        pltpu.make_async_copy(tab_ref.at[pl.ds(0, NS)], buf.at[slot], sem.at[slot]).wait()

    issue(0, 0)
    lane = lax.broadcasted_iota(jnp.int32, (NS, PEER_TT), 1)

    def token(t, carry):
        slot = t % 2

        @pl.when(t + 1 < PEER_TT)
        def _():
            issue(t + 1, 1 - slot)

        wait(slot)
        hrow = h_ref[t]
        hlo, hhi = hrow[0:8], hrow[8:16]
        for jb in range(NS // 8):
            parts = []
            for u in range(8):
                lo, hi = _unpack_words(buf[slot, jb * 8 + u, 0])
                parts.append(lo * hlo + hi * hhi)
            part_ref[jb * 8:(jb + 1) * 8, :] = _sum_sublanes_8(parts)
        act = jnp.sum(part_ref[...], axis=1, keepdims=True)
        gate = jnp.sum(jnp.where(lane == t, g_ref[...], 0.0), axis=1, keepdims=True)
        w = gate * (0.5 * act * (1.0 + lax.erf(act * 0.7071067811865476)))
        w_ref[...] = jnp.broadcast_to(w, (NS, 128))
        acc_lo = jnp.zeros((8, 128), f32)
        acc_hi = jnp.zeros((8, 128), f32)
        for j in range(NS):
            lo, hi = _unpack_words(buf[slot, j, 1])
            wj = jnp.broadcast_to(w_ref[pl.ds(j, 1), :], (8, 128))
            acc_lo = acc_lo + wj * lo
            acc_hi = acc_hi + wj * hi
        o_ref[t] = jnp.concatenate([acc_lo, acc_hi], axis=0)
        return carry

    lax.fori_loop(0, PEER_TT, token, 0)


def peer_experts(idx, table, h, gates):
    T = h.shape[0]
    tt = PEER_TT
    return pl.pallas_call(
        _peer_kernel,
        grid=(T // tt,),
        in_specs=[pl.BlockSpec((tt * PEER_SEL,), lambda i: (i,), memory_space=pltpu.SMEM),
                  pl.BlockSpec(memory_space=pl.ANY),
                  pl.BlockSpec((tt, 16, 128), lambda i: (i, 0, 0)),
                  pl.BlockSpec((PEER_SEL, tt), lambda i: (0, i))],
        out_specs=pl.BlockSpec((tt, 16, 128), lambda i: (i, 0, 0)),
        out_shape=jax.ShapeDtypeStruct((T, 16, 128), f32),
        scratch_shapes=[pltpu.VMEM((2, PEER_SEL, 2, 8, 128), jnp.uint32),
                        pltpu.SemaphoreType.DMA((2,)),
                        pltpu.VMEM((PEER_SEL, 128), f32),
                        pltpu.VMEM((PEER_SEL, 128), f32)],
        compiler_params=_params("arbitrary"),
        name="peer_experts",
    )(idx, table, h, gates)


def _pack_rows(tab):
    E, D = tab.shape
    bits = lax.bitcast_convert_type(tab.astype(bf16), jnp.uint16).astype(jnp.uint32)
    words = bits[:, :D // 2] | (bits[:, D // 2:] << 16)
    return words.reshape(E, 8, 128)


def peer_ffn(h2, wq, k1, k2, table):
    T, D = h2.shape
    q = mm(h2, wq, tn=1024)
    idx, gates = peer_route(q, k1.astype(bf16), k2.astype(bf16))
    idx_tok = idx.reshape(PEER_SEL, T).T.reshape(T * PEER_SEL)
    out = peer_experts(idx_tok, table, h2.reshape(T, 16, 128), gates.reshape(PEER_SEL, T))
    return out.reshape(T, D)


def _rms(x, g):
    return x * lax.rsqrt(jnp.mean(x * x, axis=-1, keepdims=True) + NORM_EPS) * g


def _rope_tables(n_ctx, seq, dim):
    quarter = dim // 4
    t = jnp.arange(seq, dtype=jnp.int32)
    inv_freq = ROPE_THETA ** (-jnp.arange(quarter, dtype=f32) / quarter)
    ang = jnp.concatenate([(t // GRID_W)[:, None].astype(f32) * inv_freq,
                           (t % GRID_W)[:, None].astype(f32) * inv_freq], axis=-1)
    cos = jnp.concatenate([jnp.ones((n_ctx, dim // 2), f32), jnp.cos(ang)], axis=0)
    sin = jnp.concatenate([jnp.zeros((n_ctx, dim // 2), f32), jnp.sin(ang)], axis=0)
    return cos, sin


def _rope(x, cos, sin):
    half = x.shape[-1] // 2
    x1, x2 = x[..., :half], x[..., half:]
    cs, sn = cos[None, :, None, :], sin[None, :, None, :]
    return jnp.concatenate([x1 * cs - x2 * sn, x2 * cs + x1 * sn], axis=-1)


def _ada(vec, w, b):
    n = vec.shape[0]
    a = jnp.zeros((8, vec.shape[1]), f32).at[:n].set(jax.nn.silu(vec))
    return mm(a, w, tm=8, tn=2048)[:n] + b


def kernel(x, c, ctx, c_ctx, w_mod, b_mod, norm1_g, w_in, gqa_qn_g, gqa_kn_g, gla_wa2_f, gla_ba_f, gla_wa2_b, gla_ba_b, gla_on_g, mla_qn_g, mla_wuq, mla_kvn_g, mla_wukv, w_br_gqa, w_br_gla, w_br_mla, w_out, norm2_g, peer_wq, peer_k1, peer_k2, peer_u, peer_v, final_g):
    B, S, D = x.shape
    Sc = ctx.shape[1]
    depth = w_in.shape[0]
    R = Sc + S
    ctx_tiles = Sc // ROW_TILE
    cos_a, sin_a = _rope_tables(Sc, S, HEAD_DIM)
    cos_m, sin_m = _rope_tables(Sc, S, MLA_ROPE)
    xs = jnp.concatenate([ctx, x], axis=1)

    for i in range(depth):
        need_ctx = i < depth - 1
        mods = _ada(jnp.concatenate([c_ctx[None, :], c], axis=0), w_mod[i], b_mod[i])
        mod = jnp.stack([jnp.broadcast_to(mods[0], (B, 6 * D)), mods[1:]], axis=1)
        sh1, sc1, g1, sh2, sc2, g2 = [m[:, :, None, :] for m in jnp.split(mod, 6, axis=-1)]

        def per_row(m, rows, n_ctx):
            is_ctx = (jnp.arange(rows) < n_ctx)[None, :, None]
            return jnp.where(is_ctx, m[:, 0], m[:, 1])

        h1 = norm_mod(xs, norm1_g[i], sc1, sh1, ctx_tiles=ctx_tiles, out_dtype=bf16).reshape(B * R, D)
        w = w_in[i]
        p_a = mm(h1, w[:, 0:1536], tn=768).reshape(B, R, 1536)
        p_l = mm(h1, w[:, 1536:4608], tn=1024).reshape(B, R, 3072)
        p_m = mm(h1, w[:, 4640:5664], tn=1024).reshape(B, R, 1024)
        w_small = jnp.concatenate([w[:, 4608:4640], w[:, 5664:5728], jnp.zeros((D, 32), f32)], axis=1)
        p_s = mm(h1, w_small, tn=128).reshape(B, R, 128)
        p_g = mm(h1, w[:, 5728:11872], tn=1024).reshape(B, R, 3 * D)

        aq = _rope(_rms(p_a[..., :1024].reshape(B, R, GQA_HEADS, HEAD_DIM), gqa_qn_g[i]), cos_a, sin_a)
        ak = _rope(_rms(p_a[..., 1024:1280].reshape(B, R, GQA_KV_HEADS, HEAD_DIM), gqa_kn_g[i]), cos_a, sin_a)
        aq = aq.reshape(B, R, GQA_W).astype(bf16)
        ak = ak.reshape(B, R, GQA_KV_HEADS * HEAD_DIM).astype(bf16)
        av = p_a[..., 1280:1536].astype(bf16)
        gqa = functools.partial(attention, heads=GQA_HEADS, kv_heads=GQA_KV_HEADS, dq=HEAD_DIM, dv=HEAD_DIM,
                                scale=HEAD_DIM ** -0.5)
        o_a = gqa(aq[:, Sc:], ak, av, tq=256, tk=768)
        if need_ctx:
            o_a = jnp.concatenate([gqa(aq[:, :Sc], ak[:, :Sc], av[:, :Sc], tq=Sc, tk=Sc), o_a], axis=1)

        mq = mm(_rms(p_m[..., :512], mla_qn_g[i]).reshape(B * R, MLA_Q_RANK), mla_wuq[i], tn=768)
        mq = mq.reshape(B, R, MLA_HEADS, MLA_NOPE + MLA_ROPE)
        mkv = mm(_rms(p_m[..., 512:], mla_kvn_g[i]).reshape(B * R, MLA_KV_RANK), mla_wukv[i], tn=1024)
        mkv = mkv.reshape(B, R, MLA_HEADS, MLA_NOPE + MLA_V)
        mq_rope = _rope(mq[..., MLA_NOPE:], cos_m, sin_m)
        mk_rope = _rope(p_s[..., 32:96].reshape(B, R, 1, MLA_ROPE), cos_m, sin_m)
        zpad = jnp.zeros((B, R, MLA_HEADS, MLA_QK_PAD - MLA_NOPE - MLA_ROPE), f32)
        m_q = jnp.concatenate([mq[..., :MLA_NOPE], mq_rope, zpad], axis=-1)
        m_k = jnp.concatenate([mkv[..., :MLA_NOPE], jnp.broadcast_to(mk_rope, (B, R, MLA_HEADS, MLA_ROPE)), zpad],
                              axis=-1)
        m_q = m_q.reshape(B, R, MLA_HEADS * MLA_QK_PAD).astype(bf16)
        m_k = m_k.reshape(B, R, MLA_HEADS * MLA_QK_PAD).astype(bf16)
        m_v = mkv[..., MLA_NOPE:].reshape(B, R, MLA_W).astype(bf16)
        mla = functools.partial(attention, heads=MLA_HEADS, kv_heads=MLA_HEADS, dq=MLA_QK_PAD, dv=MLA_V,
                                scale=(MLA_NOPE + MLA_ROPE) ** -0.5)
        o_m = mla(m_q[:, Sc:], m_k, m_v, tq=256, tk=768)
        if need_ctx:
            o_m = jnp.concatenate([mla(m_q[:, :Sc], m_k[:, :Sc], m_v[:, :Sc], tq=Sc, tk=Sc), o_m], axis=1)

        lr = jnp.stack([p_s[..., 0:16], p_s[..., 16:32]], axis=0)
        wa = jnp.stack([gla_wa2_f[i], gla_wa2_b[i]], axis=0).astype(bf16)
        ba = jnp.stack([gla_ba_f[i], gla_ba_b[i]], axis=0)[:, None, :]
        o_dir = gla_scans(p_l[..., 0:512], p_l[..., 512:1024], p_l[..., 1024:2048], lr, wa, ba, ctx_len=Sc)
        o_l = o_dir[0] + o_dir[1]
        lg = p_l[..., 2048:3072]
        if not need_ctx:
            o_l, lg = o_l[:, Sc:], lg[:, Sc:]
        rows = o_l.shape[1]
        o_l = (_rms(o_l.reshape(B, rows, GLA_HEADS, GLA_DV), gla_on_g[i])
               * jax.nn.silu(lg.reshape(B, rows, GLA_HEADS, GLA_DV))).reshape(B * rows, GLA_W)

        if not need_ctx:
            p_g = p_g[:, Sc:]
            xs = xs[:, Sc:]
        n_ctx = Sc if need_ctx else 0
        ga, gl, gm = [jax.nn.sigmoid(t).reshape(B * rows, D) for t in jnp.split(p_g, 3, axis=-1)]
        y = (ga * mm(o_a.reshape(B * rows, GQA_W), w_br_gqa[i], tn=1024)
             + gl * mm(o_l, w_br_gla[i], tn=1024)
             + gm * mm(o_m.reshape(B * rows, MLA_W), w_br_mla[i], tn=1024))
        y = mm(y, w_out[i], tn=1024).reshape(B, rows, D)
        xs = xs + per_row(g1, rows, n_ctx) * y

        h2 = norm_mod(xs, norm2_g[i], sc2, sh2, ctx_tiles=n_ctx // ROW_TILE, out_dtype=f32)
        table = jnp.stack([_pack_rows(peer_u[i]), _pack_rows(peer_v[i])], axis=1)
        ff = peer_ffn(h2.reshape(B * rows, D), peer_wq[i], peer_k1[i], peer_k2[i], table)
        xs = xs + per_row(g2, rows, n_ctx) * ff.reshape(B, rows, D)

    zero = jnp.zeros((B, 2, 1, D), f32)
    return norm_mod(xs, final_g, zero, zero, ctx_tiles=0, out_dtype=f32)
```

```python
import functools

import jax
import jax.numpy as jnp
from jax import lax
from jax.experimental import pallas as pl
from jax.experimental.pallas import tpu as pltpu

f32 = jnp.float32
bf16 = jnp.bfloat16

D_MODEL = 2048
GRID_W = 64
ROPE_THETA = 10000.0
NORM_EPS = 1e-6

GQA_HEADS = 8
GQA_KV_HEADS = 2
HEAD_DIM = 128
GLA_HEADS = 4
GLA_DK = 128
GLA_DV = 256
GLA_GATE_RANK = 16
GLA_TAU = 16.0
GLA_CHUNK = 64
GLA_SUB = 16
MLA_HEADS = 8
MLA_Q_RANK = 512
MLA_KV_RANK = 512
MLA_NOPE = 128
MLA_ROPE = 64
MLA_V = 128
MLA_QK_PAD = 256
GQA_W = GQA_HEADS * HEAD_DIM
GLA_W = GLA_HEADS * GLA_DV
MLA_W = MLA_HEADS * MLA_V
PEER_HEADS = 8
PEER_NKEYS = 128
PEER_QDIM = 256
PEER_HALF = PEER_QDIM // 2
PEER_TOPK = 16
PEER_SEL = PEER_HEADS * PEER_TOPK
ROUTE_HEADS_PER_STEP = 2

VMEM_LIMIT = 48 * 1024 * 1024
ROW_TILE = 256


def _params(*sem):
    return pltpu.CompilerParams(dimension_semantics=sem, vmem_limit_bytes=VMEM_LIMIT)


def _mm_kernel(a_ref, b_ref, o_ref, bq_ref):
    @pl.when(pl.program_id(1) == 0)
    def _():
        bq_ref[...] = b_ref[...].astype(bf16)

    o_ref[...] = jnp.dot(a_ref[...].astype(bf16), bq_ref[...],
                         preferred_element_type=f32).astype(o_ref.dtype)


def mm(a, b, *, tm=512, tn=None, out_dtype=f32):
    M, K = a.shape
    N = b.shape[1]
    tn = N if tn is None else tn
    tm = min(tm, M)
    assert M % tm == 0 and N % tn == 0, (M, N, tm, tn)
    return pl.pallas_call(
        _mm_kernel,
        grid=(N // tn, M // tm),
        in_specs=[pl.BlockSpec((tm, K), lambda j, i: (i, 0)),
                  pl.BlockSpec((K, tn), lambda j, i: (0, j))],
        out_specs=pl.BlockSpec((tm, tn), lambda j, i: (i, j)),
        out_shape=jax.ShapeDtypeStruct((M, N), out_dtype),
        scratch_shapes=[pltpu.VMEM((K, tn), bf16)],
        compiler_params=_params("arbitrary", "arbitrary"),
        name="mm",
    )(a, b)


def _norm_kernel(x_ref, g_ref, sc_ref, sh_ref, o_ref):
    x = x_ref[0]
    y = x * lax.rsqrt(jnp.mean(x * x, axis=-1, keepdims=True) + NORM_EPS) * g_ref[...]
    o_ref[0] = (y * (1.0 + sc_ref[0, 0]) + sh_ref[0, 0]).astype(o_ref.dtype)


def norm_mod(x, g, sc, sh, *, ctx_tiles, out_dtype):
    B, R, D = x.shape
    kind = lambda r: jnp.where(r < ctx_tiles, 0, 1)
    return pl.pallas_call(
        _norm_kernel,
        grid=(B, R // ROW_TILE),
        in_specs=[pl.BlockSpec((1, ROW_TILE, D), lambda b, r: (b, r, 0)),
                  pl.BlockSpec((1, D), lambda b, r: (0, 0)),
                  pl.BlockSpec((1, 1, 1, D), lambda b, r: (b, kind(r), 0, 0)),
                  pl.BlockSpec((1, 1, 1, D), lambda b, r: (b, kind(r), 0, 0))],
        out_specs=pl.BlockSpec((1, ROW_TILE, D), lambda b, r: (b, r, 0)),
        out_shape=jax.ShapeDtypeStruct((B, R, D), out_dtype),
        compiler_params=_params("arbitrary", "arbitrary"),
        name="norm_mod",
    )(x, g.reshape(1, D), sc, sh)


def _attn_kernel(q_ref, k_ref, v_ref, o_ref, *, tk, nk, scale):
    q = q_ref[0]
    tq = q.shape[0]
    dv = v_ref.shape[-1]

    def body(c, carry):
        m, l, acc = carry
        off = pl.multiple_of(c * tk, tk)
        kc = k_ref[0, pl.ds(off, tk), :]
        vc = v_ref[0, pl.ds(off, tk), :]
        s = lax.dot_general(q, kc, (((1,), (1,)), ((), ())), preferred_element_type=f32) * scale
        m_new = jnp.maximum(m, jnp.max(s, axis=1, keepdims=True))
        p = jnp.exp(s - m_new)
        alpha = jnp.exp(m - m_new)
        l = alpha * l + jnp.sum(p, axis=1, keepdims=True)
        acc = alpha * acc + jnp.dot(p.astype(bf16), vc, preferred_element_type=f32)
        return m_new, l, acc

    init = (jnp.full((tq, 1), -jnp.inf, f32), jnp.zeros((tq, 1), f32), jnp.zeros((tq, dv), f32))
    _, l, acc = lax.fori_loop(0, nk, body, init)
    o_ref[0] = (acc / l).astype(o_ref.dtype)


def attention(q, k, v, *, heads, kv_heads, dq, dv, scale, tq, tk):
    B, Sq, _ = q.shape
    Sk = k.shape[1]
    group = heads // kv_heads
    assert Sq % tq == 0 and Sk % tk == 0
    return pl.pallas_call(
        functools.partial(_attn_kernel, tk=tk, nk=Sk // tk, scale=scale),
        grid=(B, heads, Sq // tq),
        in_specs=[pl.BlockSpec((1, tq, dq), lambda b, h, i: (b, i, h)),
                  pl.BlockSpec((1, Sk, dq), lambda b, h, i: (b, 0, h // group)),
                  pl.BlockSpec((1, Sk, dv), lambda b, h, i: (b, 0, h // group))],
        out_specs=pl.BlockSpec((1, tq, dv), lambda b, h, i: (b, i, h)),
        out_shape=jax.ShapeDtypeStruct((B, Sq, heads * dv), f32),
        compiler_params=_params("arbitrary", "arbitrary", "arbitrary"),
        name="attention",
    )(q, k, v)


def _exact_dot_01(t01, x):
    h1 = x.astype(bf16)
    r1 = x - h1.astype(f32)
    h2 = r1.astype(bf16)
    h3 = (r1 - h2.astype(f32)).astype(bf16)
    d = lambda h: jnp.dot(t01, h, preferred_element_type=f32)
    return d(h1) + d(h2) + d(h3)


def _gla_kernel(q_ref, k_ref, v_ref, lr_ref, wa_ref, ba_ref, o_ref, state_ref, b_ref, *, rev):
    c = pl.program_id(1)
    C, SB = GLA_CHUNK, GLA_SUB
    nt = (((1,), (1,)), ((), ()))

    @pl.when(c == 0)
    def _():
        state_ref[...] = jnp.zeros_like(state_ref)

    row = lax.broadcasted_iota(jnp.int32, (C, C), 0)
    col = lax.broadcasted_iota(jnp.int32, (C, C), 1)
    tri = (row <= col) if rev else (row >= col)
    pre = jnp.dot(lr_ref[0].astype(bf16), wa_ref[...], preferred_element_type=f32) + ba_ref[...]
    log_a = (jnp.minimum(pre, 0.0) - jnp.log1p(jnp.exp(-jnp.abs(pre)))) * (1.0 / GLA_TAU)
    b_ref[...] = _exact_dot_01(tri.astype(bf16), log_a)
    end = 0 if rev else C - 1
    krow = lax.broadcasted_iota(jnp.int32, (C, GLA_DK), 0)
    lane = lax.broadcasted_iota(jnp.int32, (SB, C), 1)
    trow = lax.broadcasted_iota(jnp.int32, (SB, C), 0)

    for h in range(GLA_HEADS):
        ks = slice(h * GLA_DK, (h + 1) * GLA_DK)
        vs = slice(h * GLA_DV, (h + 1) * GLA_DV)
        qh = q_ref[0, :, ks] * (GLA_DK ** -0.5)
        kh = k_ref[0, :, ks]
        vh = v_ref[0, :, vs].astype(bf16)
        bh = b_ref[:, ks]
        b_end = b_ref[end:end + 1, ks]
        st = state_ref[h]

        o_inter = lax.dot_general((qh * jnp.exp(bh)).astype(bf16), st.astype(bf16), nt,
                                  preferred_element_type=f32)

        blocks = []
        for i in range(C // SB):
            r0 = i * SB
            qi, bi = qh[r0:r0 + SB], bh[r0:r0 + SB]
            has_earlier = i < C // SB - 1 if rev else i > 0
            if has_earlier:
                first = r0 + SB - 1 if rev else r0
                b_first = b_ref[first:first + 1, ks]
                earlier = (krow >= r0 + SB) if rev else (krow < r0)
                k_dec = jnp.where(earlier, kh * jnp.exp(jnp.minimum(b_first - bh, 0.0)), 0.0)
                q_dec = qi * jnp.exp(jnp.minimum(bi - b_first, 0.0))
                blk = lax.dot_general(q_dec.astype(bf16), k_dec.astype(bf16), nt, preferred_element_type=f32)
            else:
                blk = jnp.zeros((SB, C), f32)
            diag = jnp.zeros((SB, C), f32)
            for u in range(SB):
                s = r0 + u
                x = qi * jnp.exp(jnp.minimum(bi - b_ref[s:s + 1, ks], 0.0)) * k_ref[0, s:s + 1, ks]
                diag = jnp.where(lane == s, jnp.sum(x, axis=1, keepdims=True), diag)
            causal = (lane >= trow + r0) if rev else (lane <= trow + r0)
            blocks.append(blk + jnp.where(causal, diag, 0.0))
        scores = jnp.concatenate(blocks, axis=0)
        o_intra = jnp.dot(scores.astype(bf16), vh, preferred_element_type=f32)
        o_ref[0, :, vs] = o_inter + o_intra

        k_end = (kh * jnp.exp(b_end - bh)).astype(bf16)
        state_ref[h] = st * jnp.exp(b_end) + lax.dot_general(
            vh, k_end, (((0,), (0,)), ((), ())), preferred_element_type=f32)


def gla_scans(q, k, v, lr, wa, ba, *, ctx_len):
    B, R, _ = q.shape
    C = GLA_CHUNK
    n = R // C
    cc = ctx_len // C
    wide = GLA_HEADS * GLA_DK

    def scan(rev):
        chunk = (lambda c: jnp.where(c < cc, cc - 1 - c, n - 1 + cc - c)) if rev else (lambda c: c)
        d = int(rev)
        return pl.pallas_call(
            functools.partial(_gla_kernel, rev=rev),
            grid=(B, n),
            in_specs=[pl.BlockSpec((1, C, wide), lambda b, c: (b, chunk(c), 0)),
                      pl.BlockSpec((1, C, wide), lambda b, c: (b, chunk(c), 0)),
                      pl.BlockSpec((1, C, GLA_W), lambda b, c: (b, chunk(c), 0)),
                      pl.BlockSpec((1, C, GLA_GATE_RANK), lambda b, c: (b, chunk(c), 0)),
                      pl.BlockSpec((GLA_GATE_RANK, wide), lambda b, c: (0, 0)),
                      pl.BlockSpec((1, wide), lambda b, c: (0, 0))],
            out_specs=pl.BlockSpec((1, C, GLA_W), lambda b, c: (b, chunk(c), 0)),
            out_shape=jax.ShapeDtypeStruct((B, R, GLA_W), f32),
            scratch_shapes=[pltpu.VMEM((GLA_HEADS, GLA_DV, GLA_DK), f32),
                            pltpu.VMEM((C, wide), f32)],
            compiler_params=_params("arbitrary", "arbitrary"),
            name="gla_scan_rev" if rev else "gla_scan_fwd",
        )(q, k, v, lr[d], wa[d], ba[d])

    return scan(False), scan(True)


def _top16(s, ids):
    big = jnp.float32(1e9)
    vals, picks = [], []
    for _ in range(PEER_TOPK):
        m = jnp.max(s, axis=0, keepdims=True)
        pick = jnp.min(jnp.where(s == m, ids, big), axis=0, keepdims=True)
        vals.append(m)
        picks.append(pick)
        s = jnp.where(ids == pick, -jnp.inf, s)
    return jnp.concatenate(vals, axis=0), jnp.concatenate(picks, axis=0)


def _route_head(q, k1, k2):
    nt = (((1,), (1,)), ((), ()))
    s1 = lax.dot_general(k1, q[:, :PEER_HALF], nt, preferred_element_type=f32)
    s2 = lax.dot_general(k2, q[:, PEER_HALF:], nt, preferred_element_type=f32)
    key_id = lax.broadcasted_iota(jnp.int32, s1.shape, 0).astype(f32)
    v1, i1 = _top16(s1, key_id)
    v2, i2 = _top16(s2, key_id)
    half = PEER_TOPK // 2
    sub = lax.broadcasted_iota(jnp.int32, (PEER_TOPK, v1.shape[1]), 0).astype(f32)
    nb = lambda a: PEER_TOPK if a == 0 else half
    cand = jnp.concatenate([v1[a:a + 1] + v2[:nb(a)] for a in range(half)] + [v1[half:] + v2[0:1]], axis=0)
    cand_e = jnp.concatenate([i1[a:a + 1] * PEER_NKEYS + i2[:nb(a)] for a in range(half)]
                             + [i1[half:] * PEER_NKEYS + i2[0:1]], axis=0)
    pos = jnp.concatenate([a * PEER_TOPK + sub[:nb(a)] for a in range(half)]
                          + [(sub[:half] + half) * PEER_TOPK], axis=0)
    best, bpos = _top16(cand, pos)
    experts = [jnp.max(jnp.where(pos == bpos[r:r + 1], cand_e, -1.0), axis=0, keepdims=True)
               for r in range(PEER_TOPK)]
    e = jnp.exp(best - best[0:1])
    return jnp.concatenate(experts, axis=0).astype(jnp.int32), e / jnp.sum(e, axis=0, keepdims=True)


def _router_kernel(q_ref, k1_ref, k2_ref, idx_ref, g_ref):
    for h in range(ROUTE_HEADS_PER_STEP):
        q = q_ref[:, h * PEER_QDIM:(h + 1) * PEER_QDIM].astype(bf16)
        idx_ref[h], g_ref[h] = _route_head(q, k1_ref[h], k2_ref[h])


def peer_route(q, k1, k2):
    T = q.shape[0]
    tt = 128
    hs = ROUTE_HEADS_PER_STEP
    return pl.pallas_call(
        _router_kernel,
        grid=(T // tt, PEER_HEADS // hs),
        in_specs=[pl.BlockSpec((tt, hs * PEER_QDIM), lambda i, h: (i, h)),
                  pl.BlockSpec((hs, PEER_NKEYS, PEER_HALF), lambda i, h: (h, 0, 0)),
                  pl.BlockSpec((hs, PEER_NKEYS, PEER_HALF), lambda i, h: (h, 0, 0))],
        out_specs=[pl.BlockSpec((hs, PEER_TOPK, tt), lambda i, h: (h, 0, i)),
                   pl.BlockSpec((hs, PEER_TOPK, tt), lambda i, h: (h, 0, i))],
        out_shape=[jax.ShapeDtypeStruct((PEER_HEADS, PEER_TOPK, T), jnp.int32),
                   jax.ShapeDtypeStruct((PEER_HEADS, PEER_TOPK, T), f32)],
        compiler_params=_params("arbitrary", "arbitrary"),
        name="peer_route",
    )(q, k1, k2)


PEER_TT = 128
PEER_AHEAD = 3
PEER_SLOTS = PEER_AHEAD + 1
PEER_NEXT = 8
assert PEER_NEXT >= PEER_AHEAD and ((PEER_TT + PEER_NEXT) * PEER_SEL) % 1024 == 0
assert PEER_TT % PEER_SLOTS == 0


def _unpack_words(w):
    lo = pltpu.bitcast(w << 16, f32)
    hi = pltpu.bitcast(w & jnp.uint32(0xFFFF0000), f32)
    return lo, hi


def _sum_sublanes_8(parts):
    sub = lax.broadcasted_iota(jnp.int32, (8, 128), 0)
    dist = 4
    while len(parts) > 1:
        nxt = []
        half = len(parts) // 2
        for a in range(half):
            lo_rows = parts[a] + pltpu.roll(parts[a], 8 - dist, axis=0)
            hi_rows = parts[a + half] + pltpu.roll(parts[a + half], dist, axis=0)
            nxt.append(jnp.where((sub & dist) == 0, lo_rows, hi_rows))
        parts = nxt
        dist //= 2
    return parts[0]


def _peer_kernel(idx_ref, tab_ref, h_ref, g_ref, o_ref, *scratch):
    bufs, (sem, part_ref, w_ref) = scratch[:PEER_SLOTS], scratch[PEER_SLOTS:]
    NS = PEER_SEL
    step = pl.program_id(0)

    def issue(t, slot):
        for j in range(NS):
            e = idx_ref[t * NS + j]
            pltpu.make_async_copy(tab_ref.at[e], bufs[slot].at[:, :, j, :], sem.at[slot]).start()

    def wait(slot):
        pltpu.make_async_copy(bufs[slot], bufs[slot], sem.at[slot]).wait()

    @pl.when(step == 0)
    def _():
        for t in range(PEER_AHEAD):
            issue(t, t)

    lane = lax.broadcasted_iota(jnp.int32, (NS, PEER_TT), 1)

    def compute(t, slot):
        buf = bufs[slot]
        hrow = h_ref[t]
        hb = [jnp.broadcast_to(hrow[r:r + 1], (8, 128)) for r in range(16)]
        for jg in range(NS // 8):
            js = slice(jg * 8, (jg + 1) * 8)
            acc = None
            for s in range(8):
                lo, hi = _unpack_words(buf[0, s, js, :])
                term = lo * hb[2 * s] + hi * hb[2 * s + 1]
                acc = term if acc is None else acc + term
            part_ref[js, :] = acc
        act = jnp.sum(part_ref[...], axis=1, keepdims=True)
        gate = jnp.sum(jnp.where(lane == t, g_ref[...], 0.0), axis=1, keepdims=True)
        w = gate * (0.5 * act * (1.0 + lax.erf(act * 0.7071067811865476)))
        w_ref[...] = jnp.broadcast_to(w, (NS, 128))
        rows = []
        for s in range(8):
            acc_lo = acc_hi = None
            for jg in range(NS // 8):
                js = slice(jg * 8, (jg + 1) * 8)
                lo, hi = _unpack_words(buf[1, s, js, :])
                wv = w_ref[js, :]
                acc_lo = wv * lo if acc_lo is None else acc_lo + wv * lo
                acc_hi = wv * hi if acc_hi is None else acc_hi + wv * hi
            rows += [acc_lo, acc_hi]
        o_ref[t] = jnp.concatenate([_sum_sublanes_8(rows[0:8]), _sum_sublanes_8(rows[8:16])], axis=0)

    def group(tg, carry):
        for u in range(PEER_SLOTS):
            t = tg * PEER_SLOTS + u
            wait(u)
            issue(t + PEER_AHEAD, (u + PEER_AHEAD) % PEER_SLOTS)
            compute(t, u)
        return carry

    lax.fori_loop(0, PEER_TT // PEER_SLOTS, group, 0)

    @pl.when(step == pl.num_programs(0) - 1)
    def _():
        for t in range(PEER_AHEAD):
            wait(t)


def peer_experts(idx, table, h, gates):
    T = h.shape[0]
    tt = PEER_TT
    tiles = idx.reshape(T // tt, tt, PEER_SEL)
    idx_ext = jnp.concatenate([tiles, jnp.roll(tiles, -1, axis=0)[:, :PEER_NEXT]], axis=1).reshape(-1)
    return pl.pallas_call(
        _peer_kernel,
        grid=(T // tt,),
        in_specs=[pl.BlockSpec(((tt + PEER_NEXT) * PEER_SEL,), lambda i: (i,), memory_space=pltpu.SMEM),
                  pl.BlockSpec(memory_space=pl.ANY),
                  pl.BlockSpec((tt, 16, 128), lambda i: (i, 0, 0)),
                  pl.BlockSpec((PEER_SEL, tt), lambda i: (0, i))],
        out_specs=pl.BlockSpec((tt, 16, 128), lambda i: (i, 0, 0)),
        out_shape=jax.ShapeDtypeStruct((T, 16, 128), f32),
        scratch_shapes=[pltpu.VMEM((2, 8, PEER_SEL, 128), jnp.uint32)] * PEER_SLOTS + [
                        pltpu.SemaphoreType.DMA((PEER_SLOTS,)),
                        pltpu.VMEM((PEER_SEL, 128), f32),
                        pltpu.VMEM((PEER_SEL, 128), f32)],
        compiler_params=_params("arbitrary"),
        name="peer_experts",
    )(idx_ext, table, h, gates)


def _pack_rows(tab):
    E = tab.shape[0]
    bits = lax.bitcast_convert_type(tab.astype(bf16), jnp.uint16).astype(jnp.uint32).reshape(E, 8, 2, 128)
    return bits[:, :, 0, :] | (bits[:, :, 1, :] << 16)


def peer_ffn(h2, wq, k1, k2, table):
    T, D = h2.shape
    q = mm(h2, wq, tn=1024)
    idx, gates = peer_route(q, k1.astype(bf16), k2.astype(bf16))
    idx_tok = idx.reshape(PEER_SEL, T).T
    out = peer_experts(idx_tok, table, h2.reshape(T, 16, 128), gates.reshape(PEER_SEL, T))
    return out.reshape(T, D)


def _rms(x, g):
    return x * lax.rsqrt(jnp.mean(x * x, axis=-1, keepdims=True) + NORM_EPS) * g


def _rope_tables(n_ctx, seq, dim):
    quarter = dim // 4
    t = jnp.arange(seq, dtype=jnp.int32)
    inv_freq = ROPE_THETA ** (-jnp.arange(quarter, dtype=f32) / quarter)
    ang = jnp.concatenate([(t // GRID_W)[:, None].astype(f32) * inv_freq,
                           (t % GRID_W)[:, None].astype(f32) * inv_freq], axis=-1)
    cos = jnp.concatenate([jnp.ones((n_ctx, dim // 2), f32), jnp.cos(ang)], axis=0)
    sin = jnp.concatenate([jnp.zeros((n_ctx, dim // 2), f32), jnp.sin(ang)], axis=0)
    return cos, sin


def _rope(x, cos, sin):
    half = x.shape[-1] // 2
    x1, x2 = x[..., :half], x[..., half:]
    cs, sn = cos[None, :, None, :], sin[None, :, None, :]
    return jnp.concatenate([x1 * cs - x2 * sn, x2 * cs + x1 * sn], axis=-1)


def _ada(vec, w, b):
    n = vec.shape[0]
    a = jnp.zeros((8, vec.shape[1]), f32).at[:n].set(jax.nn.silu(vec))
    return mm(a, w, tm=8, tn=1024)[:n] + b


def kernel(x, c, ctx, c_ctx, w_mod, b_mod, norm1_g, w_in, gqa_qn_g, gqa_kn_g, gla_wa2_f, gla_ba_f, gla_wa2_b, gla_ba_b, gla_on_g, mla_qn_g, mla_wuq, mla_kvn_g, mla_wukv, w_br_gqa, w_br_gla, w_br_mla, w_out, norm2_g, peer_wq, peer_k1, peer_k2, peer_u, peer_v, final_g):
    B, S, D = x.shape
    Sc = ctx.shape[1]
    depth = w_in.shape[0]
    R = Sc + S
    ctx_tiles = Sc // ROW_TILE
    cos_a, sin_a = _rope_tables(Sc, S, HEAD_DIM)
    cos_m, sin_m = _rope_tables(Sc, S, MLA_ROPE)
    xs = jnp.concatenate([ctx, x], axis=1)

    for i in range(depth):
        need_ctx = i < depth - 1
        mods = _ada(jnp.concatenate([c_ctx[None, :], c], axis=0), w_mod[i], b_mod[i])
        mod = jnp.stack([jnp.broadcast_to(mods[0], (B, 6 * D)), mods[1:]], axis=1)
        sh1, sc1, g1, sh2, sc2, g2 = [m[:, :, None, :] for m in jnp.split(mod, 6, axis=-1)]

        def per_row(m, rows, n_ctx):
            is_ctx = (jnp.arange(rows) < n_ctx)[None, :, None]
            return jnp.where(is_ctx, m[:, 0], m[:, 1])

        h1 = norm_mod(xs, norm1_g[i], sc1, sh1, ctx_tiles=ctx_tiles, out_dtype=bf16).reshape(B * R, D)
        w = w_in[i]
        p_a = mm(h1, w[:, 0:1536], tn=768).reshape(B, R, 1536)
        p_l = mm(h1, w[:, 1536:4608], tn=1024).reshape(B, R, 3072)
        p_m = mm(h1, w[:, 4640:5664], tn=1024).reshape(B, R, 1024)
        w_small = jnp.concatenate([w[:, 4608:4640], w[:, 5664:5728], jnp.zeros((D, 32), f32)], axis=1)
        p_s = mm(h1, w_small, tn=128).reshape(B, R, 128)
        p_g = mm(h1, w[:, 5728:11872], tn=1024, out_dtype=bf16).reshape(B, R, 3 * D)

        aq = _rope(_rms(p_a[..., :1024].reshape(B, R, GQA_HEADS, HEAD_DIM), gqa_qn_g[i]), cos_a, sin_a)
        ak = _rope(_rms(p_a[..., 1024:1280].reshape(B, R, GQA_KV_HEADS, HEAD_DIM), gqa_kn_g[i]), cos_a, sin_a)
        aq = aq.reshape(B, R, GQA_W).astype(bf16)
        ak = ak.reshape(B, R, GQA_KV_HEADS * HEAD_DIM).astype(bf16)
        av = p_a[..., 1280:1536].astype(bf16)
        gqa = functools.partial(attention, heads=GQA_HEADS, kv_heads=GQA_KV_HEADS, dq=HEAD_DIM, dv=HEAD_DIM,
                                scale=HEAD_DIM ** -0.5)
        o_a = gqa(aq[:, Sc:], ak, av, tq=256, tk=768)
        if need_ctx:
            o_a = jnp.concatenate([gqa(aq[:, :Sc], ak[:, :Sc], av[:, :Sc], tq=Sc, tk=Sc), o_a], axis=1)

        mq = mm(_rms(p_m[..., :512], mla_qn_g[i]).reshape(B * R, MLA_Q_RANK), mla_wuq[i], tn=768)
        mq = mq.reshape(B, R, MLA_HEADS, MLA_NOPE + MLA_ROPE)
        mkv = mm(_rms(p_m[..., 512:], mla_kvn_g[i]).reshape(B * R, MLA_KV_RANK), mla_wukv[i], tn=1024)
        mkv = mkv.reshape(B, R, MLA_HEADS, MLA_NOPE + MLA_V)
        mq_rope = _rope(mq[..., MLA_NOPE:], cos_m, sin_m)
        mk_rope = _rope(p_s[..., 32:96].reshape(B, R, 1, MLA_ROPE), cos_m, sin_m)
        zpad = jnp.zeros((B, R, MLA_HEADS, MLA_QK_PAD - MLA_NOPE - MLA_ROPE), f32)
        m_q = jnp.concatenate([mq[..., :MLA_NOPE], mq_rope, zpad], axis=-1)
        m_k = jnp.concatenate([mkv[..., :MLA_NOPE], jnp.broadcast_to(mk_rope, (B, R, MLA_HEADS, MLA_ROPE)), zpad],
                              axis=-1)
        m_q = m_q.reshape(B, R, MLA_HEADS * MLA_QK_PAD).astype(bf16)
        m_k = m_k.reshape(B, R, MLA_HEADS * MLA_QK_PAD).astype(bf16)
        m_v = mkv[..., MLA_NOPE:].reshape(B, R, MLA_W).astype(bf16)
        mla = functools.partial(attention, heads=MLA_HEADS, kv_heads=MLA_HEADS, dq=MLA_QK_PAD, dv=MLA_V,
                                scale=(MLA_NOPE + MLA_ROPE) ** -0.5)
        o_m = mla(m_q[:, Sc:], m_k, m_v, tq=256, tk=768)
        if need_ctx:
            o_m = jnp.concatenate([mla(m_q[:, :Sc], m_k[:, :Sc], m_v[:, :Sc], tq=Sc, tk=Sc), o_m], axis=1)

        lr = jnp.stack([p_s[..., 0:16], p_s[..., 16:32]], axis=0)
        wa = jnp.stack([gla_wa2_f[i], gla_wa2_b[i]], axis=0).astype(bf16)
        ba = jnp.stack([gla_ba_f[i], gla_ba_b[i]], axis=0)[:, None, :]
        o_dir = gla_scans(p_l[..., 0:512], p_l[..., 512:1024], p_l[..., 1024:2048], lr, wa, ba, ctx_len=Sc)
        o_l = o_dir[0] + o_dir[1]
        lg = p_l[..., 2048:3072]
        if not need_ctx:
            o_l, lg = o_l[:, Sc:], lg[:, Sc:]
        rows = o_l.shape[1]
        o_l = (_rms(o_l.reshape(B, rows, GLA_HEADS, GLA_DV), gla_on_g[i])
               * jax.nn.silu(lg.reshape(B, rows, GLA_HEADS, GLA_DV))).reshape(B * rows, GLA_W)

        if not need_ctx:
            p_g = p_g[:, Sc:]
            xs = xs[:, Sc:]
        n_ctx = Sc if need_ctx else 0
        ga, gl, gm = [jax.nn.sigmoid(t.astype(f32)).reshape(B * rows, D) for t in jnp.split(p_g, 3, axis=-1)]
        y = (ga * mm(o_a.reshape(B * rows, GQA_W), w_br_gqa[i], tn=1024)
             + gl * mm(o_l, w_br_gla[i], tn=1024)
             + gm * mm(o_m.reshape(B * rows, MLA_W), w_br_mla[i], tn=1024))
        y = mm(y, w_out[i], tn=1024).reshape(B, rows, D)
        xs = xs + per_row(g1, rows, n_ctx) * y

        h2 = norm_mod(xs, norm2_g[i], sc2, sh2, ctx_tiles=n_ctx // ROW_TILE, out_dtype=f32)
        table = jnp.stack([_pack_rows(peer_u[i]), _pack_rows(peer_v[i])], axis=1)
        ff = peer_ffn(h2.reshape(B * rows, D), peer_wq[i], peer_k1[i], peer_k2[i], table)
        xs = xs + per_row(g2, rows, n_ctx) * ff.reshape(B, rows, D)

    zero = jnp.zeros((B, 2, 1, D), f32)
    return norm_mod(xs, final_g, zero, zero, ctx_tiles=0, out_dtype=f32)
```

```python
import functools

import jax
import jax.numpy as jnp
from jax import lax
from jax.experimental import pallas as pl
from jax.experimental.pallas import tpu as pltpu

f32 = jnp.float32
bf16 = jnp.bfloat16

D_MODEL = 2048
GRID_W = 64
ROPE_THETA = 10000.0
NORM_EPS = 1e-6
LOG2E = 1.4426950408889634

GQA_HEADS = 8
GQA_KV_HEADS = 2
HEAD_DIM = 128
GLA_HEADS = 4
GLA_DK = 128
GLA_DV = 256
GLA_GATE_RANK = 16
GLA_TAU = 16.0
GLA_CHUNK = 64
GLA_SUB = 16
MLA_HEADS = 8
MLA_Q_RANK = 512
MLA_KV_RANK = 512
MLA_NOPE = 128
MLA_ROPE = 64
MLA_V = 128
MLA_QK_PAD = 256
GQA_W = GQA_HEADS * HEAD_DIM
GLA_W = GLA_HEADS * GLA_DV
MLA_W = MLA_HEADS * MLA_V
PEER_HEADS = 8
PEER_NKEYS = 128
PEER_QDIM = 256
PEER_HALF = PEER_QDIM // 2
PEER_TOPK = 16
PEER_SEL = PEER_HEADS * PEER_TOPK
ROUTE_HEADS_PER_STEP = 2

VMEM_LIMIT = 48 * 1024 * 1024
ROW_TILE = 256


def _params(*sem):
    return pltpu.CompilerParams(dimension_semantics=sem, vmem_limit_bytes=VMEM_LIMIT)


def _mm_kernel(a_ref, b_ref, o_ref, bq_ref):
    @pl.when(pl.program_id(1) == 0)
    def _():
        bq_ref[...] = b_ref[...].astype(bf16)

    o_ref[...] = jnp.dot(a_ref[...].astype(bf16), bq_ref[...],
                         preferred_element_type=f32).astype(o_ref.dtype)


def mm(a, b, *, tm=512, tn=None, out_dtype=f32):
    M, K = a.shape
    N = b.shape[1]
    tn = N if tn is None else tn
    tm = min(tm, M)
    assert M % tm == 0 and N % tn == 0, (M, N, tm, tn)
    return pl.pallas_call(
        _mm_kernel,
        grid=(N // tn, M // tm),
        in_specs=[pl.BlockSpec((tm, K), lambda j, i: (i, 0)),
                  pl.BlockSpec((K, tn), lambda j, i: (0, j))],
        out_specs=pl.BlockSpec((tm, tn), lambda j, i: (i, j)),
        out_shape=jax.ShapeDtypeStruct((M, N), out_dtype),
        scratch_shapes=[pltpu.VMEM((K, tn), bf16)],
        compiler_params=_params("arbitrary", "arbitrary"),
        name="mm",
    )(a, b)


def _norm_kernel(x_ref, g_ref, sc_ref, sh_ref, o_ref):
    x = x_ref[0]
    y = x * lax.rsqrt(jnp.mean(x * x, axis=-1, keepdims=True) + NORM_EPS) * g_ref[...]
    o_ref[0] = (y * (1.0 + sc_ref[0, 0]) + sh_ref[0, 0]).astype(o_ref.dtype)


def norm_mod(x, g, sc, sh, *, ctx_tiles, out_dtype):
    B, R, D = x.shape
    kind = lambda r: jnp.where(r < ctx_tiles, 0, 1)
    return pl.pallas_call(
        _norm_kernel,
        grid=(B, R // ROW_TILE),
        in_specs=[pl.BlockSpec((1, ROW_TILE, D), lambda b, r: (b, r, 0)),
                  pl.BlockSpec((1, D), lambda b, r: (0, 0)),
                  pl.BlockSpec((1, 1, 1, D), lambda b, r: (b, kind(r), 0, 0)),
                  pl.BlockSpec((1, 1, 1, D), lambda b, r: (b, kind(r), 0, 0))],
        out_specs=pl.BlockSpec((1, ROW_TILE, D), lambda b, r: (b, r, 0)),
        out_shape=jax.ShapeDtypeStruct((B, R, D), out_dtype),
        compiler_params=_params("arbitrary", "arbitrary"),
        name="norm_mod",
    )(x, g.reshape(1, D), sc, sh)


ATT_ROWS = 256


def _attn_kernel(q_ref, k_ref, v_ref, o_ref, m_ref, acc_ref, *, tk, nk, group, splits, dq, dv):
    R = ATT_ROWS
    chains = [(g, r) for g in range(group) for r in range(splits)]
    m_ref[...] = jnp.full(m_ref.shape, -jnp.inf, f32)
    acc_ref[...] = jnp.zeros(acc_ref.shape, f32)

    def body(c, carry):
        off = pl.multiple_of(c * tk, tk)
        kc = k_ref[0, pl.ds(off, tk), :]
        vc = v_ref[0, pl.ds(off, tk), :]
        for n, (g, r) in enumerate(chains):
            q = q_ref[0, r * R:(r + 1) * R, g * dq:(g + 1) * dq]
            s = lax.dot_general(q, kc, (((1,), (1,)), ((), ())), preferred_element_type=f32)
            m_old = m_ref[n]
            m_new = jnp.maximum(m_old, jnp.max(s, axis=1, keepdims=True))
            p = jnp.exp2(s - m_new)
            alpha = jnp.exp2(m_old - m_new)
            acc_ref[n] = alpha * acc_ref[n] + jnp.dot(p.astype(bf16), vc, preferred_element_type=f32)
            m_ref[n] = m_new
        return carry

    lax.fori_loop(0, nk, body, 0)
    for n, (g, r) in enumerate(chains):
        acc = acc_ref[n]
        o_ref[0, r * R:(r + 1) * R, g * dv:(g + 1) * dv] = (acc[:, :dv] / acc[:, dv:dv + 1]).astype(o_ref.dtype)


def attention(q, k, v, *, heads, kv_heads, dq, dv, tq, tk, q_start=0, q_rows=None):
    B, Sq, _ = q.shape
    Sk = k.shape[1]
    q_rows = Sq - q_start if q_rows is None else q_rows
    group = heads // kv_heads
    splits = tq // ATT_ROWS
    assert q_rows % tq == 0 and q_start % tq == 0 and Sk % tk == 0 and tq % ATT_ROWS == 0
    first = q_start // tq
    chains = group * splits
    return pl.pallas_call(
        functools.partial(_attn_kernel, tk=tk, nk=Sk // tk, group=group, splits=splits, dq=dq, dv=dv),
        grid=(B, kv_heads, q_rows // tq),
        in_specs=[pl.BlockSpec((1, tq, group * dq), lambda b, h, i: (b, i + first, h)),
                  pl.BlockSpec((1, Sk, dq), lambda b, h, i: (b, 0, h)),
                  pl.BlockSpec((1, Sk, 2 * dv), lambda b, h, i: (b, 0, h))],
        out_specs=pl.BlockSpec((1, tq, group * dv), lambda b, h, i: (b, i, h)),
        out_shape=jax.ShapeDtypeStruct((B, q_rows, heads * dv), f32),
        scratch_shapes=[pltpu.VMEM((chains, ATT_ROWS, 1), f32),
                        pltpu.VMEM((chains, ATT_ROWS, 2 * dv), f32)],
        compiler_params=_params("arbitrary", "arbitrary", "arbitrary"),
        name="attention",
    )(q, k, v)


def _exact_dot_01(t01, x):
    h1 = x.astype(bf16)
    r1 = x - h1.astype(f32)
    h2 = r1.astype(bf16)
    h3 = (r1 - h2.astype(f32)).astype(bf16)
    d = lambda h: jnp.dot(t01, h, preferred_element_type=f32)
    return d(h1) + d(h2) + d(h3)


def _gla_kernel(q_ref, k_ref, v_ref, lr_ref, wa_ref, ba_ref, o_ref, state_ref, b_ref, *, rev):
    c = pl.program_id(1)
    C, SB = GLA_CHUNK, GLA_SUB
    nt = (((1,), (1,)), ((), ()))

    @pl.when(c == 0)
    def _():
        state_ref[...] = jnp.zeros_like(state_ref)

    row = lax.broadcasted_iota(jnp.int32, (C, C), 0)
    col = lax.broadcasted_iota(jnp.int32, (C, C), 1)
    tri = (row <= col) if rev else (row >= col)
    pre = jnp.dot(lr_ref[0].astype(bf16), wa_ref[...], preferred_element_type=f32) + ba_ref[...]
    log_a = (jnp.minimum(pre, 0.0) - jnp.log1p(jnp.exp(-jnp.abs(pre)))) * (1.0 / GLA_TAU)
    b_ref[...] = _exact_dot_01(tri.astype(bf16), log_a)
    end = 0 if rev else C - 1
    krow = lax.broadcasted_iota(jnp.int32, (C, GLA_DK), 0)
    lane = lax.broadcasted_iota(jnp.int32, (SB, C), 1)
    trow = lax.broadcasted_iota(jnp.int32, (SB, C), 0)

    for h in range(GLA_HEADS):
        ks = slice(h * GLA_DK, (h + 1) * GLA_DK)
        vs = slice(h * GLA_DV, (h + 1) * GLA_DV)
        qh = q_ref[0, :, ks] * (GLA_DK ** -0.5)
        kh = k_ref[0, :, ks]
        vh = v_ref[0, :, vs].astype(bf16)
        bh = b_ref[:, ks]
        b_end = b_ref[end:end + 1, ks]
        st = state_ref[h]

        o_inter = lax.dot_general((qh * jnp.exp(bh)).astype(bf16), st.astype(bf16), nt,
                                  preferred_element_type=f32)

        blocks = []
        for i in range(C // SB):
            r0 = i * SB
            qi, bi = qh[r0:r0 + SB], bh[r0:r0 + SB]
            has_earlier = i < C // SB - 1 if rev else i > 0
            if has_earlier:
                first = r0 + SB - 1 if rev else r0
                b_first = b_ref[first:first + 1, ks]
                earlier = (krow >= r0 + SB) if rev else (krow < r0)
                k_dec = jnp.where(earlier, kh * jnp.exp(jnp.minimum(b_first - bh, 0.0)), 0.0)
                q_dec = qi * jnp.exp(jnp.minimum(bi - b_first, 0.0))
                blk = lax.dot_general(q_dec.astype(bf16), k_dec.astype(bf16), nt, preferred_element_type=f32)
            else:
                blk = jnp.zeros((SB, C), f32)
            diag = jnp.zeros((SB, C), f32)
            for u in range(SB):
                s = r0 + u
                x = qi * jnp.exp(jnp.minimum(bi - b_ref[s:s + 1, ks], 0.0)) * k_ref[0, s:s + 1, ks]
                diag = jnp.where(lane == s, jnp.sum(x, axis=1, keepdims=True), diag)
            causal = (lane >= trow + r0) if rev else (lane <= trow + r0)
            blocks.append(blk + jnp.where(causal, diag, 0.0))
        scores = jnp.concatenate(blocks, axis=0)
        o_intra = jnp.dot(scores.astype(bf16), vh, preferred_element_type=f32)
        o_ref[0, :, vs] = o_inter + o_intra

        k_end = (kh * jnp.exp(b_end - bh)).astype(bf16)
        state_ref[h] = st * jnp.exp(b_end) + lax.dot_general(
            vh, k_end, (((0,), (0,)), ((), ())), preferred_element_type=f32)


def gla_scans(q, k, v, lr, wa, ba, *, ctx_len):
    B, R, _ = q.shape
    C = GLA_CHUNK
    n = R // C
    cc = ctx_len // C
    wide = GLA_HEADS * GLA_DK

    def scan(rev):
        chunk = (lambda c: jnp.where(c < cc, cc - 1 - c, n - 1 + cc - c)) if rev else (lambda c: c)
        d = int(rev)
        return pl.pallas_call(
            functools.partial(_gla_kernel, rev=rev),
            grid=(B, n),
            in_specs=[pl.BlockSpec((1, C, wide), lambda b, c: (b, chunk(c), 0)),
                      pl.BlockSpec((1, C, wide), lambda b, c: (b, chunk(c), 0)),
                      pl.BlockSpec((1, C, GLA_W), lambda b, c: (b, chunk(c), 0)),
                      pl.BlockSpec((1, C, GLA_GATE_RANK), lambda b, c: (b, chunk(c), 0)),
                      pl.BlockSpec((GLA_GATE_RANK, wide), lambda b, c: (0, 0)),
                      pl.BlockSpec((1, wide), lambda b, c: (0, 0))],
            out_specs=pl.BlockSpec((1, C, GLA_W), lambda b, c: (b, chunk(c), 0)),
            out_shape=jax.ShapeDtypeStruct((B, R, GLA_W), f32),
            scratch_shapes=[pltpu.VMEM((GLA_HEADS, GLA_DV, GLA_DK), f32),
                            pltpu.VMEM((C, wide), f32)],
            compiler_params=_params("arbitrary", "arbitrary"),
            name="gla_scan_rev" if rev else "gla_scan_fwd",
        )(q, k, v, lr[d], wa[d], ba[d])

    return scan(False), scan(True)


def _top16(s, ids):
    big = jnp.float32(1e9)
    vals, picks = [], []
    for _ in range(PEER_TOPK):
        m = jnp.max(s, axis=0, keepdims=True)
        pick = jnp.min(jnp.where(s == m, ids, big), axis=0, keepdims=True)
        vals.append(m)
        picks.append(pick)
        s = jnp.where(ids == pick, -jnp.inf, s)
    return jnp.concatenate(vals, axis=0), jnp.concatenate(picks, axis=0)


def _route_head(q, k1, k2):
    nt = (((1,), (1,)), ((), ()))
    s1 = lax.dot_general(k1, q[:, :PEER_HALF], nt, preferred_element_type=f32)
    s2 = lax.dot_general(k2, q[:, PEER_HALF:], nt, preferred_element_type=f32)
    key_id = lax.broadcasted_iota(jnp.int32, s1.shape, 0).astype(f32)
    v1, i1 = _top16(s1, key_id)
    v2, i2 = _top16(s2, key_id)
    half = PEER_TOPK // 2
    sub = lax.broadcasted_iota(jnp.int32, (PEER_TOPK, v1.shape[1]), 0).astype(f32)
    nb = lambda a: PEER_TOPK if a == 0 else half
    cand = jnp.concatenate([v1[a:a + 1] + v2[:nb(a)] for a in range(half)] + [v1[half:] + v2[0:1]], axis=0)
    cand_e = jnp.concatenate([i1[a:a + 1] * PEER_NKEYS + i2[:nb(a)] for a in range(half)]
                             + [i1[half:] * PEER_NKEYS + i2[0:1]], axis=0)
    pos = jnp.concatenate([a * PEER_TOPK + sub[:nb(a)] for a in range(half)]
                          + [(sub[:half] + half) * PEER_TOPK], axis=0)
    best, bpos = _top16(cand, pos)
    experts = [jnp.max(jnp.where(pos == bpos[r:r + 1], cand_e, -1.0), axis=0, keepdims=True)
               for r in range(PEER_TOPK)]
    e = jnp.exp(best - best[0:1])
    return jnp.concatenate(experts, axis=0).astype(jnp.int32), e / jnp.sum(e, axis=0, keepdims=True)


def _router_kernel(q_ref, k1_ref, k2_ref, idx_ref, g_ref):
    for h in range(ROUTE_HEADS_PER_STEP):
        q = q_ref[:, h * PEER_QDIM:(h + 1) * PEER_QDIM].astype(bf16)
        idx_ref[h], g_ref[h] = _route_head(q, k1_ref[h], k2_ref[h])


def peer_route(q, k1, k2):
    T = q.shape[0]
    tt = 128
    hs = ROUTE_HEADS_PER_STEP
    return pl.pallas_call(
        _router_kernel,
        grid=(T // tt, PEER_HEADS // hs),
        in_specs=[pl.BlockSpec((tt, hs * PEER_QDIM), lambda i, h: (i, h)),
                  pl.BlockSpec((hs, PEER_NKEYS, PEER_HALF), lambda i, h: (h, 0, 0)),
                  pl.BlockSpec((hs, PEER_NKEYS, PEER_HALF), lambda i, h: (h, 0, 0))],
        out_specs=[pl.BlockSpec((hs, PEER_TOPK, tt), lambda i, h: (h, 0, i)),
                   pl.BlockSpec((hs, PEER_TOPK, tt), lambda i, h: (h, 0, i))],
        out_shape=[jax.ShapeDtypeStruct((PEER_HEADS, PEER_TOPK, T), jnp.int32),
                   jax.ShapeDtypeStruct((PEER_HEADS, PEER_TOPK, T), f32)],
        compiler_params=_params("arbitrary", "arbitrary"),
        name="peer_route",
    )(q, k1, k2)


PEER_TT = 128
PEER_AHEAD = 3
PEER_SLOTS = PEER_AHEAD + 1
PEER_NEXT = 8
assert PEER_NEXT >= PEER_AHEAD and ((PEER_TT + PEER_NEXT) * PEER_SEL) % 1024 == 0
assert PEER_TT % PEER_SLOTS == 0


def _unpack_words(w):
    lo = pltpu.bitcast(w << 16, f32)
    hi = pltpu.bitcast(w & jnp.uint32(0xFFFF0000), f32)
    return lo, hi


def _sum_sublanes_8(parts):
    sub = lax.broadcasted_iota(jnp.int32, (8, 128), 0)
    dist = 4
    while len(parts) > 1:
        nxt = []
        half = len(parts) // 2
        for a in range(half):
            lo_rows = parts[a] + pltpu.roll(parts[a], 8 - dist, axis=0)
            hi_rows = parts[a + half] + pltpu.roll(parts[a + half], dist, axis=0)
            nxt.append(jnp.where((sub & dist) == 0, lo_rows, hi_rows))
        parts = nxt
        dist //= 2
    return parts[0]


def _peer_kernel(idx_ref, tab_ref, h_ref, g_ref, o_ref, *scratch):
    bufs, (sem, part_ref, w_ref) = scratch[:PEER_SLOTS], scratch[PEER_SLOTS:]
    NS = PEER_SEL
    step = pl.program_id(0)

    def issue(t, slot):
        for j in range(NS):
            e = idx_ref[t * NS + j]
            pltpu.make_async_copy(tab_ref.at[e], bufs[slot].at[:, :, j, :], sem.at[slot]).start(priority=j % 2)

    def wait(slot):
        pltpu.make_async_copy(bufs[slot], bufs[slot], sem.at[slot]).wait()

    @pl.when(step == 0)
    def _():
        for t in range(PEER_AHEAD):
            issue(t, t)

    lane = lax.broadcasted_iota(jnp.int32, (NS, PEER_TT), 1)

    def compute(t, slot):
        buf = bufs[slot]
        hrow = h_ref[t]
        hb = [jnp.broadcast_to(hrow[r:r + 1], (8, 128)) for r in range(16)]
        for jg in range(NS // 8):
            js = slice(jg * 8, (jg + 1) * 8)
            acc = None
            for s in range(8):
                lo, hi = _unpack_words(buf[0, s, js, :])
                term = lo * hb[2 * s] + hi * hb[2 * s + 1]
                acc = term if acc is None else acc + term
            part_ref[js, :] = acc
        act = jnp.sum(part_ref[...], axis=1, keepdims=True)
        gate = jnp.sum(jnp.where(lane == t, g_ref[...], 0.0), axis=1, keepdims=True)
        w = gate * (0.5 * act * (1.0 + lax.erf(act * 0.7071067811865476)))
        w_ref[...] = jnp.broadcast_to(w, (NS, 128))
        rows = []
        for s in range(8):
            acc_lo = acc_hi = None
            for jg in range(NS // 8):
                js = slice(jg * 8, (jg + 1) * 8)
                lo, hi = _unpack_words(buf[1, s, js, :])
                wv = w_ref[js, :]
                acc_lo = wv * lo if acc_lo is None else acc_lo + wv * lo
                acc_hi = wv * hi if acc_hi is None else acc_hi + wv * hi
            rows += [acc_lo, acc_hi]
        o_ref[t] = jnp.concatenate([_sum_sublanes_8(rows[0:8]), _sum_sublanes_8(rows[8:16])], axis=0)

    def group(tg, carry):
        for u in range(PEER_SLOTS):
            t = tg * PEER_SLOTS + u
            wait(u)
            issue(t + PEER_AHEAD, (u + PEER_AHEAD) % PEER_SLOTS)
            compute(t, u)
        return carry

    lax.fori_loop(0, PEER_TT // PEER_SLOTS, group, 0)

    @pl.when(step == pl.num_programs(0) - 1)
    def _():
        for t in range(PEER_AHEAD):
            wait(t)


def peer_experts(idx, table, h, gates):
    T = h.shape[0]
    tt = PEER_TT
    tiles = idx.reshape(T // tt, tt, PEER_SEL)
    idx_ext = jnp.concatenate([tiles, jnp.roll(tiles, -1, axis=0)[:, :PEER_NEXT]], axis=1).reshape(-1)
    return pl.pallas_call(
        _peer_kernel,
        grid=(T // tt,),
        in_specs=[pl.BlockSpec(((tt + PEER_NEXT) * PEER_SEL,), lambda i: (i,), memory_space=pltpu.SMEM),
                  pl.BlockSpec(memory_space=pl.ANY),
                  pl.BlockSpec((tt, 16, 128), lambda i: (i, 0, 0)),
                  pl.BlockSpec((PEER_SEL, tt), lambda i: (0, i))],
        out_specs=pl.BlockSpec((tt, 16, 128), lambda i: (i, 0, 0)),
        out_shape=jax.ShapeDtypeStruct((T, 16, 128), f32),
        scratch_shapes=[pltpu.VMEM((2, 8, PEER_SEL, 128), jnp.uint32)] * PEER_SLOTS + [
                        pltpu.SemaphoreType.DMA((PEER_SLOTS,)),
                        pltpu.VMEM((PEER_SEL, 128), f32),
                        pltpu.VMEM((PEER_SEL, 128), f32)],
        compiler_params=_params("arbitrary"),
        name="peer_experts",
    )(idx_ext, table, h, gates)


def _pack_rows(tab):
    E = tab.shape[0]
    bits = lax.bitcast_convert_type(tab.astype(bf16), jnp.uint16).astype(jnp.uint32).reshape(E, 8, 2, 128)
    return bits[:, :, 0, :] | (bits[:, :, 1, :] << 16)


def peer_ffn(h2, wq, k1, k2, table):
    T, D = h2.shape
    q = mm(h2, wq, tn=1024)
    idx, gates = peer_route(q, k1.astype(bf16), k2.astype(bf16))
    idx_tok = idx.reshape(PEER_SEL, T).T
    out = peer_experts(idx_tok, table, h2.reshape(T, 16, 128), gates.reshape(PEER_SEL, T))
    return out.reshape(T, D)


def _rms(x, g):
    return x * lax.rsqrt(jnp.mean(x * x, axis=-1, keepdims=True) + NORM_EPS) * g


def _rope_tables(n_ctx, seq, dim):
    quarter = dim // 4
    t = jnp.arange(seq, dtype=jnp.int32)
    inv_freq = ROPE_THETA ** (-jnp.arange(quarter, dtype=f32) / quarter)
    ang = jnp.concatenate([(t // GRID_W)[:, None].astype(f32) * inv_freq,
                           (t % GRID_W)[:, None].astype(f32) * inv_freq], axis=-1)
    cos = jnp.concatenate([jnp.ones((n_ctx, dim // 2), f32), jnp.cos(ang)], axis=0)
    sin = jnp.concatenate([jnp.zeros((n_ctx, dim // 2), f32), jnp.sin(ang)], axis=0)
    return cos, sin


def _rope(x, cos, sin):
    half = x.shape[-1] // 2
    x1, x2 = x[..., :half], x[..., half:]
    cs, sn = cos[None, :, None, :], sin[None, :, None, :]
    return jnp.concatenate([x1 * cs - x2 * sn, x2 * cs + x1 * sn], axis=-1)


def _with_ones_column(v):
    B, R, H, dv = v.shape
    extra = jnp.zeros((B, R, H, dv), f32).at[..., 0].set(1.0)
    return jnp.concatenate([v, extra], axis=-1).reshape(B, R, H * 2 * dv).astype(bf16)


def _ada(vec, w, b):
    n = vec.shape[0]
    a = jnp.zeros((8, vec.shape[1]), f32).at[:n].set(jax.nn.silu(vec))
    return mm(a, w, tm=8, tn=1024)[:n] + b


def kernel(x, c, ctx, c_ctx, w_mod, b_mod, norm1_g, w_in, gqa_qn_g, gqa_kn_g, gla_wa2_f, gla_ba_f, gla_wa2_b, gla_ba_b, gla_on_g, mla_qn_g, mla_wuq, mla_kvn_g, mla_wukv, w_br_gqa, w_br_gla, w_br_mla, w_out, norm2_g, peer_wq, peer_k1, peer_k2, peer_u, peer_v, final_g):
    B, S, D = x.shape
    Sc = ctx.shape[1]
    depth = w_in.shape[0]
    R = Sc + S
    ctx_tiles = Sc // ROW_TILE
    cos_a, sin_a = _rope_tables(Sc, S, HEAD_DIM)
    cos_m, sin_m = _rope_tables(Sc, S, MLA_ROPE)
    xs = jnp.concatenate([ctx, x], axis=1)

    for i in range(depth):
        need_ctx = i < depth - 1
        mods = _ada(jnp.concatenate([c_ctx[None, :], c], axis=0), w_mod[i], b_mod[i])
        mod = jnp.stack([jnp.broadcast_to(mods[0], (B, 6 * D)), mods[1:]], axis=1)
        sh1, sc1, g1, sh2, sc2, g2 = [m[:, :, None, :] for m in jnp.split(mod, 6, axis=-1)]

        def per_row(m, rows, n_ctx):
            is_ctx = (jnp.arange(rows) < n_ctx)[None, :, None]
            return jnp.where(is_ctx, m[:, 0], m[:, 1])

        h1 = norm_mod(xs, norm1_g[i], sc1, sh1, ctx_tiles=ctx_tiles, out_dtype=bf16).reshape(B * R, D)
        w = w_in[i]
        p_a = mm(h1, w[:, 0:1536], tn=768).reshape(B, R, 1536)
        p_l = mm(h1, w[:, 1536:4608], tn=1024).reshape(B, R, 3072)
        p_m = mm(h1, w[:, 4640:5664], tn=1024).reshape(B, R, 1024)
        w_small = jnp.concatenate([w[:, 4608:4640], w[:, 5664:5728], jnp.zeros((D, 32), f32)], axis=1)
        p_s = mm(h1, w_small, tn=128).reshape(B, R, 128)
        p_g = mm(h1, w[:, 5728:11872], tn=1024, out_dtype=bf16).reshape(B, R, 3 * D)

        aq = _rope(_rms(p_a[..., :1024].reshape(B, R, GQA_HEADS, HEAD_DIM), gqa_qn_g[i]), cos_a, sin_a)
        ak = _rope(_rms(p_a[..., 1024:1280].reshape(B, R, GQA_KV_HEADS, HEAD_DIM), gqa_kn_g[i]), cos_a, sin_a)
        aq = (aq * (HEAD_DIM ** -0.5 * LOG2E)).reshape(B, R, GQA_W).astype(bf16)
        ak = ak.reshape(B, R, GQA_KV_HEADS * HEAD_DIM).astype(bf16)
        av = _with_ones_column(p_a[..., 1280:1536].reshape(B, R, GQA_KV_HEADS, HEAD_DIM))
        gqa = functools.partial(attention, heads=GQA_HEADS, kv_heads=GQA_KV_HEADS, dq=HEAD_DIM, dv=HEAD_DIM)
        o_a = gqa(aq, ak, av, tq=ATT_ROWS, tk=768, q_start=Sc)
        if need_ctx:
            o_a = jnp.concatenate([gqa(aq, ak[:, :Sc], av[:, :Sc], tq=Sc, tk=Sc, q_rows=Sc), o_a], axis=1)

        mq = mm(_rms(p_m[..., :512], mla_qn_g[i]).reshape(B * R, MLA_Q_RANK), mla_wuq[i], tn=768)
        mq = mq.reshape(B, R, MLA_HEADS, MLA_NOPE + MLA_ROPE)
        mkv = mm(_rms(p_m[..., 512:], mla_kvn_g[i]).reshape(B * R, MLA_KV_RANK), mla_wukv[i], tn=1024)
        mkv = mkv.reshape(B, R, MLA_HEADS, MLA_NOPE + MLA_V)
        mq_rope = _rope(mq[..., MLA_NOPE:], cos_m, sin_m)
        mk_rope = _rope(p_s[..., 32:96].reshape(B, R, 1, MLA_ROPE), cos_m, sin_m)
        zpad = jnp.zeros((B, R, MLA_HEADS, MLA_QK_PAD - MLA_NOPE - MLA_ROPE), f32)
        m_q = jnp.concatenate([mq[..., :MLA_NOPE], mq_rope, zpad], axis=-1) * ((MLA_NOPE + MLA_ROPE) ** -0.5 * LOG2E)
        m_k = jnp.concatenate([mkv[..., :MLA_NOPE], jnp.broadcast_to(mk_rope, (B, R, MLA_HEADS, MLA_ROPE)), zpad],
                              axis=-1)
        m_q = m_q.reshape(B, R, MLA_HEADS * MLA_QK_PAD).astype(bf16)
        m_k = m_k.reshape(B, R, MLA_HEADS * MLA_QK_PAD).astype(bf16)
        m_v = _with_ones_column(mkv[..., MLA_NOPE:])
        mla = functools.partial(attention, heads=MLA_HEADS, kv_heads=MLA_HEADS, dq=MLA_QK_PAD, dv=MLA_V)
        o_m = mla(m_q[:, Sc:], m_k, m_v, tq=4 * ATT_ROWS, tk=768)
        if need_ctx:
            o_m = jnp.concatenate([mla(m_q, m_k[:, :Sc], m_v[:, :Sc], tq=Sc, tk=Sc, q_rows=Sc), o_m], axis=1)

        lr = jnp.stack([p_s[..., 0:16], p_s[..., 16:32]], axis=0)
        wa = jnp.stack([gla_wa2_f[i], gla_wa2_b[i]], axis=0).astype(bf16)
        ba = jnp.stack([gla_ba_f[i], gla_ba_b[i]], axis=0)[:, None, :]
        o_dir = gla_scans(p_l[..., 0:512], p_l[..., 512:1024], p_l[..., 1024:2048], lr, wa, ba, ctx_len=Sc)
        o_l = o_dir[0] + o_dir[1]
        lg = p_l[..., 2048:3072]
        if not need_ctx:
            o_l, lg = o_l[:, Sc:], lg[:, Sc:]
        rows = o_l.shape[1]
        o_l = (_rms(o_l.reshape(B, rows, GLA_HEADS, GLA_DV), gla_on_g[i])
               * jax.nn.silu(lg.reshape(B, rows, GLA_HEADS, GLA_DV))).reshape(B * rows, GLA_W)

        if not need_ctx:
            p_g = p_g[:, Sc:]
            xs = xs[:, Sc:]
        n_ctx = Sc if need_ctx else 0
        ga, gl, gm = [jax.nn.sigmoid(t.astype(f32)).reshape(B * rows, D) for t in jnp.split(p_g, 3, axis=-1)]
        y = (ga * mm(o_a.reshape(B * rows, GQA_W), w_br_gqa[i], tn=1024)
             + gl * mm(o_l, w_br_gla[i], tn=1024)
             + gm * mm(o_m.reshape(B * rows, MLA_W), w_br_mla[i], tn=1024))
        y = mm(y, w_out[i], tn=1024).reshape(B, rows, D)
        xs = xs + per_row(g1, rows, n_ctx) * y

        h2 = norm_mod(xs, norm2_g[i], sc2, sh2, ctx_tiles=n_ctx // ROW_TILE, out_dtype=f32)
        table = jnp.stack([_pack_rows(peer_u[i]), _pack_rows(peer_v[i])], axis=1)
        ff = peer_ffn(h2.reshape(B * rows, D), peer_wq[i], peer_k1[i], peer_k2[i], table)
        xs = xs + per_row(g2, rows, n_ctx) * ff.reshape(B, rows, D)

    zero = jnp.zeros((B, 2, 1, D), f32)
    return norm_mod(xs, final_g, zero, zero, ctx_tiles=0, out_dtype=f32)
```

```python
import functools

import jax
import jax.numpy as jnp
from jax import lax
from jax.experimental import pallas as pl
from jax.experimental.pallas import tpu as pltpu

f32 = jnp.float32
bf16 = jnp.bfloat16

D_MODEL = 2048
GRID_W = 64
ROPE_THETA = 10000.0
NORM_EPS = 1e-6
LOG2E = 1.4426950408889634

GQA_HEADS = 8
GQA_KV_HEADS = 2
HEAD_DIM = 128
GLA_HEADS = 4
GLA_DK = 128
GLA_DV = 256
GLA_GATE_RANK = 16
GLA_TAU = 16.0
GLA_CHUNK = 64
GLA_SUB = 16
MLA_HEADS = 8
MLA_Q_RANK = 512
MLA_KV_RANK = 512
MLA_NOPE = 128
MLA_ROPE = 64
MLA_V = 128
MLA_QK_PAD = 256
GQA_W = GQA_HEADS * HEAD_DIM
GLA_W = GLA_HEADS * GLA_DV
MLA_W = MLA_HEADS * MLA_V
PEER_HEADS = 8
PEER_NKEYS = 128
PEER_QDIM = 256
PEER_HALF = PEER_QDIM // 2
PEER_TOPK = 16
PEER_SEL = PEER_HEADS * PEER_TOPK
ROUTE_HEADS_PER_STEP = 2

VMEM_LIMIT = 48 * 1024 * 1024
ROW_TILE = 256


def _params(*sem):
    return pltpu.CompilerParams(dimension_semantics=sem, vmem_limit_bytes=VMEM_LIMIT)


def _mm_kernel(a_ref, b_ref, o_ref, bq_ref):
    @pl.when(pl.program_id(1) == 0)
    def _():
        bq_ref[...] = b_ref[...].astype(bf16)

    o_ref[...] = jnp.dot(a_ref[...].astype(bf16), bq_ref[...],
                         preferred_element_type=f32).astype(o_ref.dtype)


def mm(a, b, *, tm=512, tn=None, out_dtype=f32):
    M, K = a.shape
    N = b.shape[1]
    tn = N if tn is None else tn
    tm = min(tm, M)
    assert M % tm == 0 and N % tn == 0, (M, N, tm, tn)
    return pl.pallas_call(
        _mm_kernel,
        grid=(N // tn, M // tm),
        in_specs=[pl.BlockSpec((tm, K), lambda j, i: (i, 0)),
                  pl.BlockSpec((K, tn), lambda j, i: (0, j))],
        out_specs=pl.BlockSpec((tm, tn), lambda j, i: (i, j)),
        out_shape=jax.ShapeDtypeStruct((M, N), out_dtype),
        scratch_shapes=[pltpu.VMEM((K, tn), bf16)],
        compiler_params=_params("arbitrary", "arbitrary"),
        name="mm",
    )(a, b)


FEAT_ROWS = D_MODEL // 128


def _norm_kernel(*refs, has_res, emit_x, emit_tiled):
    refs = list(refs)
    x_ref = refs.pop(0)
    x = x_ref[0]
    if has_res:
        ff_ref, gate_ref = refs.pop(0), refs.pop(0)
        ff = jnp.concatenate([ff_ref[pl.ds(r, ROW_TILE, stride=FEAT_ROWS), :] for r in range(FEAT_ROWS)], axis=1)
        x = x + gate_ref[0, 0] * ff
    g_ref, sc_ref, sh_ref = refs.pop(0), refs.pop(0), refs.pop(0)
    if emit_x:
        refs.pop(0)[0] = x
    y = x * lax.rsqrt(jnp.mean(x * x, axis=-1, keepdims=True) + NORM_EPS) * g_ref[...]
    h = y * (1.0 + sc_ref[0, 0]) + sh_ref[0, 0]
    h_ref = refs.pop(0)
    h_ref[0] = h.astype(h_ref.dtype)
    if emit_tiled:
        ht_ref = refs.pop(0)
        for r in range(FEAT_ROWS):
            ht_ref[pl.ds(r, ROW_TILE, stride=FEAT_ROWS), :] = h[:, r * 128:(r + 1) * 128]


def norm_mod(x, g, sc, sh, *, ctx_tiles, out_dtype, res=None, res_gate=None, emit_x=False, emit_tiled=False):
    B, R, D = x.shape
    nr = R // ROW_TILE
    kind = lambda r: jnp.where(r < ctx_tiles, 0, 1)
    row_spec = pl.BlockSpec((1, ROW_TILE, D), lambda b, r: (b, r, 0))
    mod_spec = pl.BlockSpec((1, 1, 1, D), lambda b, r: (b, kind(r), 0, 0))
    tile_spec = pl.BlockSpec((ROW_TILE * FEAT_ROWS, 128), lambda b, r: (b * nr + r, 0))
    args, in_specs = [x], [row_spec]
    if res is not None:
        args += [res, res_gate]
        in_specs += [tile_spec, mod_spec]
    args += [g.reshape(1, D), sc, sh]
    in_specs += [pl.BlockSpec((1, D), lambda b, r: (0, 0)), mod_spec, mod_spec]
    out_specs, out_shape = [], []
    if emit_x:
        out_specs.append(row_spec)
        out_shape.append(jax.ShapeDtypeStruct((B, R, D), f32))
    out_specs.append(row_spec)
    out_shape.append(jax.ShapeDtypeStruct((B, R, D), out_dtype))
    if emit_tiled:
        out_specs.append(tile_spec)
        out_shape.append(jax.ShapeDtypeStruct((B * R * FEAT_ROWS, 128), f32))
    return pl.pallas_call(
        functools.partial(_norm_kernel, has_res=res is not None, emit_x=emit_x, emit_tiled=emit_tiled),
        grid=(B, nr),
        in_specs=in_specs,
        out_specs=out_specs,
        out_shape=out_shape,
        compiler_params=_params("arbitrary", "arbitrary"),
        name="norm_mod",
    )(*args)


MERGE_TN = 512


def _merge_kernel(oa_ref, of_ref, ob_ref, lg_ref, om_ref, ga_ref, gl_ref, gm_ref, wa_ref, wl_ref, wm_ref, gn_ref,
                  o_ref, wa_q, wl_q, wm_q):
    @pl.when((pl.program_id(1) == 0) & (pl.program_id(2) == 0))
    def _():
        wa_q[...] = wa_ref[...].astype(bf16)
        wl_q[...] = wl_ref[...].astype(bf16)
        wm_q[...] = wm_ref[...].astype(bf16)

    o = of_ref[0] + ob_ref[0]
    lg = lg_ref[0]
    heads = []
    for h in range(GLA_HEADS):
        oh = o[:, h * GLA_DV:(h + 1) * GLA_DV]
        heads.append(oh * lax.rsqrt(jnp.mean(oh * oh, axis=-1, keepdims=True) + NORM_EPS) * gn_ref[...])
    o_l = jnp.concatenate(heads, axis=1) * (lg * jax.nn.sigmoid(lg))
    dot = lambda a, w: jnp.dot(a.astype(bf16), w[...], preferred_element_type=f32)
    sig = lambda r: jax.nn.sigmoid(r[0].astype(f32))
    y = (sig(ga_ref) * dot(oa_ref[0], wa_q) + sig(gl_ref) * dot(o_l, wl_q) + sig(gm_ref) * dot(om_ref[0], wm_q))
    o_ref[0] = y.astype(o_ref.dtype)


def merge_branches(o_a, o_f, o_b, p_l, o_m, p_g, wa, wl, wm, gn, *, row0):
    B, rows, _ = o_a.shape
    D = wa.shape[1]
    tm, tn = ROW_TILE, MERGE_TN
    nb = D // tn
    r0 = row0 // tm
    own = lambda w: pl.BlockSpec((1, tm, w), lambda j, b, i: (b, i, 0))
    full = lambda w, cb: pl.BlockSpec((1, tm, w), lambda j, b, i: (b, i + r0, cb))
    gate = lambda k: pl.BlockSpec((1, tm, tn), lambda j, b, i: (b, i + r0, k * nb + j))
    wspec = pl.BlockSpec((GQA_W, tn), lambda j, b, i: (0, j))
    return pl.pallas_call(
        _merge_kernel,
        grid=(nb, B, rows // tm),
        in_specs=[own(GQA_W), full(GLA_W, 0), full(GLA_W, 0), full(GLA_W, 2), own(MLA_W),
                  gate(0), gate(1), gate(2), wspec, wspec, wspec,
                  pl.BlockSpec((1, GLA_DV), lambda j, b, i: (0, 0))],
        out_specs=pl.BlockSpec((1, tm, tn), lambda j, b, i: (b, i, j)),
        out_shape=jax.ShapeDtypeStruct((B, rows, D), bf16),
        scratch_shapes=[pltpu.VMEM((GQA_W, tn), bf16)] * 3,
        compiler_params=_params("arbitrary", "arbitrary", "arbitrary"),
        name="merge_branches",
    )(o_a, o_f, o_b, p_l, o_m, p_g, p_g, p_g, wa, wl, wm, gn.reshape(1, GLA_DV))


def _mm_res_kernel(y_ref, w_ref, x_ref, g_ref, o_ref, wq_ref):
    @pl.when((pl.program_id(1) == 0) & (pl.program_id(2) == 0))
    def _():
        wq_ref[...] = w_ref[...].astype(bf16)

    o_ref[0] = x_ref[0] + g_ref[0, 0] * jnp.dot(y_ref[0], wq_ref[...], preferred_element_type=f32)


def mm_residual(y, w, x, gate, *, ctx_tiles, tn=1024):
    B, rows, K = y.shape
    N = w.shape[1]
    tm = ROW_TILE
    kind = lambda i: jnp.where(i < ctx_tiles, 0, 1)
    return pl.pallas_call(
        _mm_res_kernel,
        grid=(N // tn, B, rows // tm),
        in_specs=[pl.BlockSpec((1, tm, K), lambda j, b, i: (b, i, 0)),
                  pl.BlockSpec((K, tn), lambda j, b, i: (0, j)),
                  pl.BlockSpec((1, tm, tn), lambda j, b, i: (b, i, j)),
                  pl.BlockSpec((1, 1, 1, tn), lambda j, b, i: (b, kind(i), 0, j))],
        out_specs=pl.BlockSpec((1, tm, tn), lambda j, b, i: (b, i, j)),
        out_shape=jax.ShapeDtypeStruct((B, rows, N), f32),
        scratch_shapes=[pltpu.VMEM((K, tn), bf16)],
        compiler_params=_params("arbitrary", "arbitrary", "arbitrary"),
        name="mm_residual",
    )(y, w, x, gate)


ATT_ROWS = 256


def _attn_kernel(q_ref, k_ref, v_ref, o_ref, m_ref, acc_ref, *, tk, nk, group, splits, dq, dv):
    R = ATT_ROWS
    chains = [(g, r) for g in range(group) for r in range(splits)]
    m_ref[...] = jnp.full(m_ref.shape, -jnp.inf, f32)
    acc_ref[...] = jnp.zeros(acc_ref.shape, f32)

    def body(c, carry):
        off = pl.multiple_of(c * tk, tk)
        kc = k_ref[0, pl.ds(off, tk), :]
        vc = v_ref[0, pl.ds(off, tk), :]
        for n, (g, r) in enumerate(chains):
            q = q_ref[0, r * R:(r + 1) * R, g * dq:(g + 1) * dq]
            s = lax.dot_general(q, kc, (((1,), (1,)), ((), ())), preferred_element_type=f32)
            m_old = m_ref[n]
            m_new = jnp.maximum(m_old, jnp.max(s, axis=1, keepdims=True))
            p = jnp.exp2(s - m_new)
            alpha = jnp.exp2(m_old - m_new)
            acc_ref[n] = alpha * acc_ref[n] + jnp.dot(p.astype(bf16), vc, preferred_element_type=f32)
            m_ref[n] = m_new
        return carry

    lax.fori_loop(0, nk, body, 0)
    for n, (g, r) in enumerate(chains):
        acc = acc_ref[n]
        o_ref[0, r * R:(r + 1) * R, g * dv:(g + 1) * dv] = (acc[:, :dv] / acc[:, dv:dv + 1]).astype(o_ref.dtype)


def attention(q, k, v, *, heads, kv_heads, dq, dv, tq, tk, q_start=0, q_rows=None):
    B, Sq, _ = q.shape
    Sk = k.shape[1]
    q_rows = Sq - q_start if q_rows is None else q_rows
    group = heads // kv_heads
    splits = tq // ATT_ROWS
    assert q_rows % tq == 0 and q_start % tq == 0 and Sk % tk == 0 and tq % ATT_ROWS == 0
    first = q_start // tq
    chains = group * splits
    return pl.pallas_call(
        functools.partial(_attn_kernel, tk=tk, nk=Sk // tk, group=group, splits=splits, dq=dq, dv=dv),
        grid=(B, kv_heads, q_rows // tq),
        in_specs=[pl.BlockSpec((1, tq, group * dq), lambda b, h, i: (b, i + first, h)),
                  pl.BlockSpec((1, Sk, dq), lambda b, h, i: (b, 0, h)),
                  pl.BlockSpec((1, Sk, 2 * dv), lambda b, h, i: (b, 0, h))],
        out_specs=pl.BlockSpec((1, tq, group * dv), lambda b, h, i: (b, i, h)),
        out_shape=jax.ShapeDtypeStruct((B, q_rows, heads * dv), f32),
        scratch_shapes=[pltpu.VMEM((chains, ATT_ROWS, 1), f32),
                        pltpu.VMEM((chains, ATT_ROWS, 2 * dv), f32)],
        compiler_params=_params("arbitrary", "arbitrary", "arbitrary"),
        name="attention",
    )(q, k, v)


def _exact_dot_01(t01, x):
    h1 = x.astype(bf16)
    r1 = x - h1.astype(f32)
    h2 = r1.astype(bf16)
    h3 = (r1 - h2.astype(f32)).astype(bf16)
    d = lambda h: jnp.dot(t01, h, preferred_element_type=f32)
    return d(h1) + d(h2) + d(h3)


def _gla_kernel(q_ref, k_ref, v_ref, lr_ref, wa_ref, ba_ref, o_ref, state_ref, b_ref, *, rev):
    c = pl.program_id(1)
    C, SB = GLA_CHUNK, GLA_SUB
    nt = (((1,), (1,)), ((), ()))

    @pl.when(c == 0)
    def _():
        state_ref[...] = jnp.zeros_like(state_ref)

    row = lax.broadcasted_iota(jnp.int32, (C, C), 0)
    col = lax.broadcasted_iota(jnp.int32, (C, C), 1)
    tri = (row <= col) if rev else (row >= col)
    lr = lr_ref[0][:, int(rev) * GLA_GATE_RANK:(int(rev) + 1) * GLA_GATE_RANK]
    pre = jnp.dot(lr.astype(bf16), wa_ref[...], preferred_element_type=f32) + ba_ref[...]
    log_a = (jnp.minimum(pre, 0.0) - jnp.log1p(jnp.exp(-jnp.abs(pre)))) * (1.0 / GLA_TAU)
    b_ref[...] = _exact_dot_01(tri.astype(bf16), log_a)
    end = 0 if rev else C - 1
    krow = lax.broadcasted_iota(jnp.int32, (C, GLA_DK), 0)
    lane = lax.broadcasted_iota(jnp.int32, (SB, C), 1)
    trow = lax.broadcasted_iota(jnp.int32, (SB, C), 0)

    for h in range(GLA_HEADS):
        ks = slice(h * GLA_DK, (h + 1) * GLA_DK)
        vs = slice(h * GLA_DV, (h + 1) * GLA_DV)
        qh = q_ref[0, :, ks] * (GLA_DK ** -0.5)
        kh = k_ref[0, :, ks]
        vh = v_ref[0, :, vs].astype(bf16)
        bh = b_ref[:, ks]
        b_end = b_ref[end:end + 1, ks]
        st = state_ref[h]

        o_inter = lax.dot_general((qh * jnp.exp(bh)).astype(bf16), st.astype(bf16), nt,
                                  preferred_element_type=f32)

        blocks = []
        for i in range(C // SB):
            r0 = i * SB
            qi, bi = qh[r0:r0 + SB], bh[r0:r0 + SB]
            has_earlier = i < C // SB - 1 if rev else i > 0
            if has_earlier:
                first = r0 + SB - 1 if rev else r0
                b_first = b_ref[first:first + 1, ks]
                earlier = (krow >= r0 + SB) if rev else (krow < r0)
                k_dec = jnp.where(earlier, kh * jnp.exp(jnp.minimum(b_first - bh, 0.0)), 0.0)
                q_dec = qi * jnp.exp(jnp.minimum(bi - b_first, 0.0))
                blk = lax.dot_general(q_dec.astype(bf16), k_dec.astype(bf16), nt, preferred_element_type=f32)
            else:
                blk = jnp.zeros((SB, C), f32)
            diag = jnp.zeros((SB, C), f32)
            for u in range(SB):
                s = r0 + u
                x = qi * jnp.exp(jnp.minimum(bi - b_ref[s:s + 1, ks], 0.0)) * k_ref[0, s:s + 1, ks]
                diag = jnp.where(lane == s, jnp.sum(x, axis=1, keepdims=True), diag)
            causal = (lane >= trow + r0) if rev else (lane <= trow + r0)
            blocks.append(blk + jnp.where(causal, diag, 0.0))
        scores = jnp.concatenate(blocks, axis=0)
        o_intra = jnp.dot(scores.astype(bf16), vh, preferred_element_type=f32)
        o_ref[0, :, vs] = o_inter + o_intra

        k_end = (kh * jnp.exp(b_end - bh)).astype(bf16)
        state_ref[h] = st * jnp.exp(b_end) + lax.dot_general(
            vh, k_end, (((0,), (0,)), ((), ())), preferred_element_type=f32)


def gla_scans(p_l, p_s, wa, ba, *, ctx_len):
    B, R, _ = p_l.shape
    C = GLA_CHUNK
    n = R // C
    cc = ctx_len // C
    wide = GLA_HEADS * GLA_DK
    assert GLA_W == 2 * wide

    def scan(rev):
        chunk = (lambda c: jnp.where(c < cc, cc - 1 - c, n - 1 + cc - c)) if rev else (lambda c: c)
        d = int(rev)
        return pl.pallas_call(
            functools.partial(_gla_kernel, rev=rev),
            grid=(B, n),
            in_specs=[pl.BlockSpec((1, C, wide), lambda b, c: (b, chunk(c), 0)),
                      pl.BlockSpec((1, C, wide), lambda b, c: (b, chunk(c), 1)),
                      pl.BlockSpec((1, C, GLA_W), lambda b, c: (b, chunk(c), 1)),
                      pl.BlockSpec((1, C, 128), lambda b, c: (b, chunk(c), 0)),
                      pl.BlockSpec((GLA_GATE_RANK, wide), lambda b, c: (0, 0)),
                      pl.BlockSpec((1, wide), lambda b, c: (0, 0))],
            out_specs=pl.BlockSpec((1, C, GLA_W), lambda b, c: (b, chunk(c), 0)),
            out_shape=jax.ShapeDtypeStruct((B, R, GLA_W), f32),
            scratch_shapes=[pltpu.VMEM((GLA_HEADS, GLA_DV, GLA_DK), f32),
                            pltpu.VMEM((C, wide), f32)],
            compiler_params=_params("arbitrary", "arbitrary"),
            name="gla_scan_rev" if rev else "gla_scan_fwd",
        )(p_l, p_l, p_l, p_s, wa[d], ba[d])

    return scan(False), scan(True)


def _top16(s, ids):
    big = jnp.float32(1e9)
    vals, picks = [], []
    for _ in range(PEER_TOPK):
        m = jnp.max(s, axis=0, keepdims=True)
        pick = jnp.min(jnp.where(s == m, ids, big), axis=0, keepdims=True)
        vals.append(m)
        picks.append(pick)
        s = jnp.where(ids == pick, -jnp.inf, s)
    return jnp.concatenate(vals, axis=0), jnp.concatenate(picks, axis=0)


def _route_head(q, k1, k2):
    nt = (((1,), (1,)), ((), ()))
    s1 = lax.dot_general(k1, q[:, :PEER_HALF], nt, preferred_element_type=f32)
    s2 = lax.dot_general(k2, q[:, PEER_HALF:], nt, preferred_element_type=f32)
    key_id = lax.broadcasted_iota(jnp.int32, s1.shape, 0).astype(f32)
    v1, i1 = _top16(s1, key_id)
    v2, i2 = _top16(s2, key_id)
    half = PEER_TOPK // 2
    sub = lax.broadcasted_iota(jnp.int32, (PEER_TOPK, v1.shape[1]), 0).astype(f32)
    nb = lambda a: PEER_TOPK if a == 0 else half
    cand = jnp.concatenate([v1[a:a + 1] + v2[:nb(a)] for a in range(half)] + [v1[half:] + v2[0:1]], axis=0)
    cand_e = jnp.concatenate([i1[a:a + 1] * PEER_NKEYS + i2[:nb(a)] for a in range(half)]
                             + [i1[half:] * PEER_NKEYS + i2[0:1]], axis=0)
    pos = jnp.concatenate([a * PEER_TOPK + sub[:nb(a)] for a in range(half)]
                          + [(sub[:half] + half) * PEER_TOPK], axis=0)
    best, bpos = _top16(cand, pos)
    experts = [jnp.max(jnp.where(pos == bpos[r:r + 1], cand_e, -1.0), axis=0, keepdims=True)
               for r in range(PEER_TOPK)]
    e = jnp.exp(best - best[0:1])
    return jnp.concatenate(experts, axis=0).astype(jnp.int32), e / jnp.sum(e, axis=0, keepdims=True)


def _router_kernel(q_ref, k1_ref, k2_ref, idx_ref, g_ref):
    for h in range(ROUTE_HEADS_PER_STEP):
        q = q_ref[:, h * PEER_QDIM:(h + 1) * PEER_QDIM].astype(bf16)
        idx_ref[h], g_ref[h] = _route_head(q, k1_ref[h], k2_ref[h])


def peer_route(q, k1, k2):
    T = q.shape[0]
    tt = 128
    hs = ROUTE_HEADS_PER_STEP
    return pl.pallas_call(
        _router_kernel,
        grid=(T // tt, PEER_HEADS // hs),
        in_specs=[pl.BlockSpec((tt, hs * PEER_QDIM), lambda i, h: (i, h)),
                  pl.BlockSpec((hs, PEER_NKEYS, PEER_HALF), lambda i, h: (h, 0, 0)),
                  pl.BlockSpec((hs, PEER_NKEYS, PEER_HALF), lambda i, h: (h, 0, 0))],
        out_specs=[pl.BlockSpec((hs, PEER_TOPK, tt), lambda i, h: (h, 0, i)),
                   pl.BlockSpec((hs, PEER_TOPK, tt), lambda i, h: (h, 0, i))],
        out_shape=[jax.ShapeDtypeStruct((PEER_HEADS, PEER_TOPK, T), jnp.int32),
                   jax.ShapeDtypeStruct((PEER_HEADS, PEER_TOPK, T), f32)],
        compiler_params=_params("arbitrary", "arbitrary"),
        name="peer_route",
    )(q, k1, k2)


PEER_TT = 128
PEER_AHEAD = 3
PEER_SLOTS = PEER_AHEAD + 1
PEER_NEXT = 8
assert PEER_NEXT >= PEER_AHEAD and ((PEER_TT + PEER_NEXT) * PEER_SEL) % 1024 == 0
assert PEER_TT % PEER_SLOTS == 0


def _unpack_words(w):
    lo = pltpu.bitcast(w << 16, f32)
    hi = pltpu.bitcast(w & jnp.uint32(0xFFFF0000), f32)
    return lo, hi


def _sum_sublanes_8(parts):
    sub = lax.broadcasted_iota(jnp.int32, (8, 128), 0)
    dist = 4
    while len(parts) > 1:
        nxt = []
        half = len(parts) // 2
        for a in range(half):
            lo_rows = parts[a] + pltpu.roll(parts[a], 8 - dist, axis=0)
            hi_rows = parts[a + half] + pltpu.roll(parts[a + half], dist, axis=0)
            nxt.append(jnp.where((sub & dist) == 0, lo_rows, hi_rows))
        parts = nxt
        dist //= 2
    return parts[0]


def _peer_kernel(idx_ref, tab_ref, h_ref, g_ref, o_ref, *scratch):
    bufs, (sem, part_ref, w_ref) = scratch[:PEER_SLOTS], scratch[PEER_SLOTS:]
    NS = PEER_SEL
    step = pl.program_id(0)

    def issue(t, slot):
        for j in range(NS):
            e = idx_ref[t * NS + j]
            pltpu.make_async_copy(tab_ref.at[e], bufs[slot].at[:, :, j, :], sem.at[slot]).start(priority=j % 2)

    def wait(slot):
        pltpu.make_async_copy(bufs[slot], bufs[slot], sem.at[slot]).wait()

    @pl.when(step == 0)
    def _():
        for t in range(PEER_AHEAD):
            issue(t, t)

    lane = lax.broadcasted_iota(jnp.int32, (NS, PEER_TT), 1)

    def compute(t, slot):
        buf = bufs[slot]
        hrow = h_ref[t]
        hb = [jnp.broadcast_to(hrow[r:r + 1], (8, 128)) for r in range(16)]
        for jg in range(NS // 8):
            js = slice(jg * 8, (jg + 1) * 8)
            acc = None
            for s in range(8):
                lo, hi = _unpack_words(buf[0, s, js, :])
                term = lo * hb[2 * s] + hi * hb[2 * s + 1]
                acc = term if acc is None else acc + term
            part_ref[js, :] = acc
        act = jnp.sum(part_ref[...], axis=1, keepdims=True)
        gate = jnp.sum(jnp.where(lane == t, g_ref[...], 0.0), axis=1, keepdims=True)
        w = gate * (0.5 * act * (1.0 + lax.erf(act * 0.7071067811865476)))
        w_ref[...] = jnp.broadcast_to(w, (NS, 128))
        rows = []
        for s in range(8):
            acc_lo = acc_hi = None
            for jg in range(NS // 8):
                js = slice(jg * 8, (jg + 1) * 8)
                lo, hi = _unpack_words(buf[1, s, js, :])
                wv = w_ref[js, :]
                acc_lo = wv * lo if acc_lo is None else acc_lo + wv * lo
                acc_hi = wv * hi if acc_hi is None else acc_hi + wv * hi
            rows += [acc_lo, acc_hi]
        o_ref[t] = jnp.concatenate([_sum_sublanes_8(rows[0:8]), _sum_sublanes_8(rows[8:16])], axis=0)

    def group(tg, carry):
        for u in range(PEER_SLOTS):
            t = tg * PEER_SLOTS + u
            wait(u)
            issue(t + PEER_AHEAD, (u + PEER_AHEAD) % PEER_SLOTS)
            compute(t, u)
        return carry

    lax.fori_loop(0, PEER_TT // PEER_SLOTS, group, 0)

    @pl.when(step == pl.num_programs(0) - 1)
    def _():
        for t in range(PEER_AHEAD):
            wait(t)


def peer_experts(idx, table, h, gates):
    T = h.shape[0]
    tt = PEER_TT
    tiles = idx.reshape(T // tt, tt, PEER_SEL)
    idx_ext = jnp.concatenate([tiles, jnp.roll(tiles, -1, axis=0)[:, :PEER_NEXT]], axis=1).reshape(-1)
    return pl.pallas_call(
        _peer_kernel,
        grid=(T // tt,),
        in_specs=[pl.BlockSpec(((tt + PEER_NEXT) * PEER_SEL,), lambda i: (i,), memory_space=pltpu.SMEM),
                  pl.BlockSpec(memory_space=pl.ANY),
                  pl.BlockSpec((tt, 16, 128), lambda i: (i, 0, 0)),
                  pl.BlockSpec((PEER_SEL, tt), lambda i: (0, i))],
        out_specs=pl.BlockSpec((tt, 16, 128), lambda i: (i, 0, 0)),
        out_shape=jax.ShapeDtypeStruct((T, 16, 128), f32),
        scratch_shapes=[pltpu.VMEM((2, 8, PEER_SEL, 128), jnp.uint32)] * PEER_SLOTS + [
                        pltpu.SemaphoreType.DMA((PEER_SLOTS,)),
                        pltpu.VMEM((PEER_SEL, 128), f32),
                        pltpu.VMEM((PEER_SEL, 128), f32)],
        compiler_params=_params("arbitrary"),
        name="peer_experts",
    )(idx_ext, table, h, gates)


def _pack_rows(tab):
    E = tab.shape[0]
    bits = lax.bitcast_convert_type(tab.astype(bf16), jnp.uint16).astype(jnp.uint32).reshape(E, 8, 2, 128)
    return bits[:, :, 0, :] | (bits[:, :, 1, :] << 16)


def peer_ffn(h2, h2_tiles, wq, k1, k2, table):
    T = h2.shape[0]
    q = mm(h2, wq, tn=1024)
    idx, gates = peer_route(q, k1.astype(bf16), k2.astype(bf16))
    idx_tok = idx.reshape(PEER_SEL, T).T
    return peer_experts(idx_tok, table, h2_tiles, gates.reshape(PEER_SEL, T))


def _rms(x, g):
    return x * lax.rsqrt(jnp.mean(x * x, axis=-1, keepdims=True) + NORM_EPS) * g


def _rope_tables(n_ctx, seq, dim):
    quarter = dim // 4
    t = jnp.arange(seq, dtype=jnp.int32)
    inv_freq = ROPE_THETA ** (-jnp.arange(quarter, dtype=f32) / quarter)
    ang = jnp.concatenate([(t // GRID_W)[:, None].astype(f32) * inv_freq,
                           (t % GRID_W)[:, None].astype(f32) * inv_freq], axis=-1)
    cos = jnp.concatenate([jnp.ones((n_ctx, dim // 2), f32), jnp.cos(ang)], axis=0)
    sin = jnp.concatenate([jnp.zeros((n_ctx, dim // 2), f32), jnp.sin(ang)], axis=0)
    return cos, sin


def _rope(x, cos, sin):
    half = x.shape[-1] // 2
    x1, x2 = x[..., :half], x[..., half:]
    cs, sn = cos[None, :, None, :], sin[None, :, None, :]
    return jnp.concatenate([x1 * cs - x2 * sn, x2 * cs + x1 * sn], axis=-1)


def _with_ones_column(v):
    B, R, H, dv = v.shape
    extra = jnp.zeros((B, R, H, dv), f32).at[..., 0].set(1.0)
    return jnp.concatenate([v, extra], axis=-1).reshape(B, R, H * 2 * dv).astype(bf16)


def _ada(vec, w, b):
    n = vec.shape[0]
    a = jnp.zeros((8, vec.shape[1]), f32).at[:n].set(jax.nn.silu(vec))
    return mm(a, w, tm=8, tn=1024)[:n] + b


def kernel(x, c, ctx, c_ctx, w_mod, b_mod, norm1_g, w_in, gqa_qn_g, gqa_kn_g, gla_wa2_f, gla_ba_f, gla_wa2_b, gla_ba_b, gla_on_g, mla_qn_g, mla_wuq, mla_kvn_g, mla_wukv, w_br_gqa, w_br_gla, w_br_mla, w_out, norm2_g, peer_wq, peer_k1, peer_k2, peer_u, peer_v, final_g):
    B, S, D = x.shape
    Sc = ctx.shape[1]
    depth = w_in.shape[0]
    R = Sc + S
    ctx_tiles = Sc // ROW_TILE
    cos_a, sin_a = _rope_tables(Sc, S, HEAD_DIM)
    cos_m, sin_m = _rope_tables(Sc, S, MLA_ROPE)
    xs = jnp.concatenate([ctx, x], axis=1)

    mods = []
    for i in range(depth):
        m = _ada(jnp.concatenate([c_ctx[None, :], c], axis=0), w_mod[i], b_mod[i])
        m = jnp.stack([jnp.broadcast_to(m[0], (B, 6 * D)), m[1:]], axis=1)
        mods.append([t[:, :, None, :] for t in jnp.split(m, 6, axis=-1)])

    h1, = norm_mod(xs, norm1_g[0], mods[0][1], mods[0][0], ctx_tiles=ctx_tiles, out_dtype=bf16)
    for i in range(depth):
        need_ctx = i < depth - 1
        sh1, sc1, g1, sh2, sc2, g2 = mods[i]

        h1 = h1.reshape(B * R, D)
        w = w_in[i]
        p_a = mm(h1, w[:, 0:1536], tn=768).reshape(B, R, 1536)
        p_l = mm(h1, w[:, 1536:4608], tn=1024).reshape(B, R, 3072)
        p_m = mm(h1, w[:, 4640:5664], tn=1024).reshape(B, R, 1024)
        w_small = jnp.concatenate([w[:, 4608:4640], w[:, 5664:5728], jnp.zeros((D, 32), f32)], axis=1)
        p_s = mm(h1, w_small, tn=128).reshape(B, R, 128)
        p_g = mm(h1, w[:, 5728:11872], tn=1024, out_dtype=bf16).reshape(B, R, 3 * D)

        aq = _rope(_rms(p_a[..., :1024].reshape(B, R, GQA_HEADS, HEAD_DIM), gqa_qn_g[i]), cos_a, sin_a)
        ak = _rope(_rms(p_a[..., 1024:1280].reshape(B, R, GQA_KV_HEADS, HEAD_DIM), gqa_kn_g[i]), cos_a, sin_a)
        aq = (aq * (HEAD_DIM ** -0.5 * LOG2E)).reshape(B, R, GQA_W).astype(bf16)
        ak = ak.reshape(B, R, GQA_KV_HEADS * HEAD_DIM).astype(bf16)
        av = _with_ones_column(p_a[..., 1280:1536].reshape(B, R, GQA_KV_HEADS, HEAD_DIM))
        gqa = functools.partial(attention, heads=GQA_HEADS, kv_heads=GQA_KV_HEADS, dq=HEAD_DIM, dv=HEAD_DIM)
        o_a = gqa(aq, ak, av, tq=ATT_ROWS, tk=768, q_start=Sc)
        if need_ctx:
            o_a = jnp.concatenate([gqa(aq, ak[:, :Sc], av[:, :Sc], tq=Sc, tk=Sc, q_rows=Sc), o_a], axis=1)

        mq = mm(_rms(p_m[..., :512], mla_qn_g[i]).reshape(B * R, MLA_Q_RANK), mla_wuq[i], tn=768)
        mq = mq.reshape(B, R, MLA_HEADS, MLA_NOPE + MLA_ROPE)
        mkv = mm(_rms(p_m[..., 512:], mla_kvn_g[i]).reshape(B * R, MLA_KV_RANK), mla_wukv[i], tn=1024)
        mkv = mkv.reshape(B, R, MLA_HEADS, MLA_NOPE + MLA_V)
        mq_rope = _rope(mq[..., MLA_NOPE:], cos_m, sin_m)
        mk_rope = _rope(p_s[..., 32:96].reshape(B, R, 1, MLA_ROPE), cos_m, sin_m)
        zpad = jnp.zeros((B, R, MLA_HEADS, MLA_QK_PAD - MLA_NOPE - MLA_ROPE), f32)
        m_q = jnp.concatenate([mq[..., :MLA_NOPE], mq_rope, zpad], axis=-1) * ((MLA_NOPE + MLA_ROPE) ** -0.5 * LOG2E)
        m_k = jnp.concatenate([mkv[..., :MLA_NOPE], jnp.broadcast_to(mk_rope, (B, R, MLA_HEADS, MLA_ROPE)), zpad],
                              axis=-1)
        m_q = m_q.reshape(B, R, MLA_HEADS * MLA_QK_PAD).astype(bf16)
        m_k = m_k.reshape(B, R, MLA_HEADS * MLA_QK_PAD).astype(bf16)
        m_v = _with_ones_column(mkv[..., MLA_NOPE:])
        mla = functools.partial(attention, heads=MLA_HEADS, kv_heads=MLA_HEADS, dq=MLA_QK_PAD, dv=MLA_V)
        o_m = mla(m_q[:, Sc:], m_k, m_v, tq=4 * ATT_ROWS, tk=768)
        if need_ctx:
            o_m = jnp.concatenate([mla(m_q, m_k[:, :Sc], m_v[:, :Sc], tq=Sc, tk=Sc, q_rows=Sc), o_m], axis=1)

        wa = jnp.stack([gla_wa2_f[i], gla_wa2_b[i]], axis=0).astype(bf16)
        ba = jnp.stack([gla_ba_f[i], gla_ba_b[i]], axis=0)[:, None, :]
        o_f, o_b = gla_scans(p_l, p_s, wa, ba, ctx_len=Sc)

        n_ctx = Sc if need_ctx else 0
        if not need_ctx:
            xs = xs[:, Sc:]
        y = merge_branches(o_a, o_f, o_b, p_l, o_m, p_g, w_br_gqa[i], w_br_gla[i], w_br_mla[i], gla_on_g[i],
                           row0=Sc - n_ctx)
        xs = mm_residual(y, w_out[i], xs, g1, ctx_tiles=n_ctx // ROW_TILE)
        rows = xs.shape[1]

        h2, h2_tiles = norm_mod(xs, norm2_g[i], sc2, sh2, ctx_tiles=n_ctx // ROW_TILE, out_dtype=bf16, emit_tiled=True)
        table = jnp.stack([_pack_rows(peer_u[i]), _pack_rows(peer_v[i])], axis=1)
        ff = peer_ffn(h2.reshape(B * rows, D), h2_tiles.reshape(B * rows, FEAT_ROWS, 128),
                      peer_wq[i], peer_k1[i], peer_k2[i], table).reshape(B * rows * FEAT_ROWS, 128)
        if need_ctx:
            xs, h1 = norm_mod(xs, norm1_g[i + 1], mods[i + 1][1], mods[i + 1][0], ctx_tiles=ctx_tiles, out_dtype=bf16,
                              res=ff, res_gate=g2, emit_x=True)
        else:
            zero = jnp.zeros((B, 2, 1, D), f32)
            out, = norm_mod(xs, final_g, zero, zero, ctx_tiles=0, out_dtype=f32, res=ff, res_gate=g2)
    return out
```

```python
import functools

import jax
import jax.numpy as jnp
from jax import lax
from jax.experimental import pallas as pl
from jax.experimental.pallas import tpu as pltpu

f32 = jnp.float32
bf16 = jnp.bfloat16

D_MODEL = 2048
GRID_W = 64
ROPE_THETA = 10000.0
NORM_EPS = 1e-6
LOG2E = 1.4426950408889634

GQA_HEADS = 8
GQA_KV_HEADS = 2
HEAD_DIM = 128
GLA_HEADS = 4
GLA_DK = 128
GLA_DV = 256
GLA_GATE_RANK = 16
GLA_TAU = 16.0
GLA_CHUNK = 64
GLA_SUB = 16
GLA_LR_COL = 64
MLA_HEADS = 8
MLA_Q_RANK = 512
MLA_KV_RANK = 512
MLA_NOPE = 128
MLA_ROPE = 64
MLA_V = 128
MLA_QK_PAD = 256
GQA_W = GQA_HEADS * HEAD_DIM
GLA_W = GLA_HEADS * GLA_DV
MLA_W = MLA_HEADS * MLA_V
PEER_HEADS = 8
PEER_NKEYS = 128
PEER_QDIM = 256
PEER_HALF = PEER_QDIM // 2
PEER_TOPK = 16
PEER_SEL = PEER_HEADS * PEER_TOPK
ROUTE_HEADS_PER_STEP = 2

VMEM_LIMIT = 48 * 1024 * 1024
ROW_TILE = 256


def _params(*sem):
    return pltpu.CompilerParams(dimension_semantics=sem, vmem_limit_bytes=VMEM_LIMIT)


def _mm_kernel(a_ref, b_ref, o_ref, bq_ref):
    @pl.when(pl.program_id(1) == 0)
    def _():
        bq_ref[...] = b_ref[...].astype(bf16)

    o_ref[...] = jnp.dot(a_ref[...].astype(bf16), bq_ref[...],
                         preferred_element_type=f32).astype(o_ref.dtype)


def mm(a, b, *, tm=512, tn=None, out_dtype=f32):
    M, K = a.shape
    N = b.shape[1]
    tn = N if tn is None else tn
    tm = min(tm, M)
    assert M % tm == 0 and N % tn == 0, (M, N, tm, tn)
    return pl.pallas_call(
        _mm_kernel,
        grid=(N // tn, M // tm),
        in_specs=[pl.BlockSpec((tm, K), lambda j, i: (i, 0)),
                  pl.BlockSpec((K, tn), lambda j, i: (0, j))],
        out_specs=pl.BlockSpec((tm, tn), lambda j, i: (i, j)),
        out_shape=jax.ShapeDtypeStruct((M, N), out_dtype),
        scratch_shapes=[pltpu.VMEM((K, tn), bf16)],
        compiler_params=_params("arbitrary", "arbitrary"),
        name="mm",
    )(a, b)


def _mm_rms_kernel(a_ref, g_ref, b_ref, o_ref, bq_ref):
    @pl.when(pl.program_id(1) == 0)
    def _():
        bq_ref[...] = b_ref[...].astype(bf16)

    a = a_ref[...]
    a = a * lax.rsqrt(jnp.mean(a * a, axis=-1, keepdims=True) + NORM_EPS) * g_ref[...]
    o_ref[...] = jnp.dot(a.astype(bf16), bq_ref[...], preferred_element_type=f32).astype(o_ref.dtype)


def mm_rms(a, a_col, g, b, *, tm=512, tn=None, out_dtype=f32):
    M = a.shape[0]
    K, N = b.shape
    tn = N if tn is None else tn
    assert M % tm == 0 and N % tn == 0
    return pl.pallas_call(
        _mm_rms_kernel,
        grid=(N // tn, M // tm),
        in_specs=[pl.BlockSpec((tm, K), lambda j, i: (i, a_col)),
                  pl.BlockSpec((1, K), lambda j, i: (0, 0)),
                  pl.BlockSpec((K, tn), lambda j, i: (0, j))],
        out_specs=pl.BlockSpec((tm, tn), lambda j, i: (i, j)),
        out_shape=jax.ShapeDtypeStruct((M, N), out_dtype),
        scratch_shapes=[pltpu.VMEM((K, tn), bf16)],
        compiler_params=_params("arbitrary", "arbitrary"),
        name="mm_rms",
    )(a, g.reshape(1, K), b)


FEAT_ROWS = D_MODEL // 128


def _norm_kernel(*refs, has_res, emit_x, emit_tiled):
    refs = list(refs)
    x_ref = refs.pop(0)
    x = x_ref[0]
    if has_res:
        ff_ref, gate_ref = refs.pop(0), refs.pop(0)
        ff = jnp.concatenate([ff_ref[pl.ds(r, ROW_TILE, stride=FEAT_ROWS), :] for r in range(FEAT_ROWS)], axis=1)
        x = x + gate_ref[0, 0] * ff
    g_ref, sc_ref, sh_ref = refs.pop(0), refs.pop(0), refs.pop(0)
    if emit_x:
        refs.pop(0)[0] = x
    y = x * lax.rsqrt(jnp.mean(x * x, axis=-1, keepdims=True) + NORM_EPS) * g_ref[...]
    h = y * (1.0 + sc_ref[0, 0]) + sh_ref[0, 0]
    h_ref = refs.pop(0)
    h_ref[0] = h.astype(h_ref.dtype)
    if emit_tiled:
        ht_ref = refs.pop(0)
        for r in range(FEAT_ROWS):
            ht_ref[pl.ds(r, ROW_TILE, stride=FEAT_ROWS), :] = h[:, r * 128:(r + 1) * 128]


def norm_mod(x, g, sc, sh, *, ctx_tiles, out_dtype, res=None, res_gate=None, emit_x=False, emit_tiled=False):
    B, R, D = x.shape
    nr = R // ROW_TILE
    kind = lambda r: jnp.where(r < ctx_tiles, 0, 1)
    row_spec = pl.BlockSpec((1, ROW_TILE, D), lambda b, r: (b, r, 0))
    mod_spec = pl.BlockSpec((1, 1, 1, D), lambda b, r: (b, kind(r), 0, 0))
    tile_spec = pl.BlockSpec((ROW_TILE * FEAT_ROWS, 128), lambda b, r: (b * nr + r, 0))
    args, in_specs = [x], [row_spec]
    if res is not None:
        args += [res, res_gate]
        in_specs += [tile_spec, mod_spec]
    args += [g.reshape(1, D), sc, sh]
    in_specs += [pl.BlockSpec((1, D), lambda b, r: (0, 0)), mod_spec, mod_spec]
    out_specs, out_shape = [], []
    if emit_x:
        out_specs.append(row_spec)
        out_shape.append(jax.ShapeDtypeStruct((B, R, D), f32))
    out_specs.append(row_spec)
    out_shape.append(jax.ShapeDtypeStruct((B, R, D), out_dtype))
    if emit_tiled:
        out_specs.append(tile_spec)
        out_shape.append(jax.ShapeDtypeStruct((B * R * FEAT_ROWS, 128), f32))
    return pl.pallas_call(
        functools.partial(_norm_kernel, has_res=res is not None, emit_x=emit_x, emit_tiled=emit_tiled),
        grid=(B, nr),
        in_specs=in_specs,
        out_specs=out_specs,
        out_shape=out_shape,
        compiler_params=_params("arbitrary", "arbitrary"),
        name="norm_mod",
    )(*args)


MERGE_TN = 512


def _merge_kernel(oa_ref, of_ref, ob_ref, lg_ref, om_ref, ga_ref, gl_ref, gm_ref, wa_ref, wl_ref, wm_ref, gn_ref,
                  o_ref, wa_q, wl_q, wm_q):
    @pl.when((pl.program_id(1) == 0) & (pl.program_id(2) == 0))
    def _():
        wa_q[...] = wa_ref[...].astype(bf16)
        wl_q[...] = wl_ref[...].astype(bf16)
        wm_q[...] = wm_ref[...].astype(bf16)

    o = of_ref[0] + ob_ref[0]
    lg = lg_ref[0]
    heads = []
    for h in range(GLA_HEADS):
        oh = o[:, h * GLA_DV:(h + 1) * GLA_DV]
        heads.append(oh * lax.rsqrt(jnp.mean(oh * oh, axis=-1, keepdims=True) + NORM_EPS) * gn_ref[...])
    o_l = jnp.concatenate(heads, axis=1) * (lg * jax.nn.sigmoid(lg))
    dot = lambda a, w: jnp.dot(a.astype(bf16), w[...], preferred_element_type=f32)
    sig = lambda r: jax.nn.sigmoid(r[0].astype(f32))
    y = (sig(ga_ref) * dot(oa_ref[0], wa_q) + sig(gl_ref) * dot(o_l, wl_q) + sig(gm_ref) * dot(om_ref[0], wm_q))
    o_ref[0] = y.astype(o_ref.dtype)


def merge_branches(o_a, o_f, o_b, p_l, o_m, p_g, wa, wl, wm, gn, *, row0):
    B, rows, _ = o_a.shape
    D = wa.shape[1]
    tm, tn = ROW_TILE, MERGE_TN
    nb = D // tn
    r0 = row0 // tm
    own = lambda w: pl.BlockSpec((1, tm, w), lambda j, b, i: (b, i, 0))
    full = lambda w, cb: pl.BlockSpec((1, tm, w), lambda j, b, i: (b, i + r0, cb))
    gate = lambda k: pl.BlockSpec((1, tm, tn), lambda j, b, i: (b, i + r0, k * nb + j))
    wspec = pl.BlockSpec((GQA_W, tn), lambda j, b, i: (0, j))
    return pl.pallas_call(
        _merge_kernel,
        grid=(nb, B, rows // tm),
        in_specs=[own(GQA_W), full(GLA_W, 0), full(GLA_W, 0), full(GLA_W, 2), own(MLA_W),
                  gate(0), gate(1), gate(2), wspec, wspec, wspec,
                  pl.BlockSpec((1, GLA_DV), lambda j, b, i: (0, 0))],
        out_specs=pl.BlockSpec((1, tm, tn), lambda j, b, i: (b, i, j)),
        out_shape=jax.ShapeDtypeStruct((B, rows, D), bf16),
        scratch_shapes=[pltpu.VMEM((GQA_W, tn), bf16)] * 3,
        compiler_params=_params("arbitrary", "arbitrary", "arbitrary"),
        name="merge_branches",
    )(o_a, o_f, o_b, p_l, o_m, p_g, p_g, p_g, wa, wl, wm, gn.reshape(1, GLA_DV))


def _mm_res_kernel(y_ref, w_ref, x_ref, g_ref, o_ref, wq_ref):
    @pl.when((pl.program_id(1) == 0) & (pl.program_id(2) == 0))
    def _():
        wq_ref[...] = w_ref[...].astype(bf16)

    o_ref[0] = x_ref[0] + g_ref[0, 0] * jnp.dot(y_ref[0], wq_ref[...], preferred_element_type=f32)


def mm_residual(y, w, x, gate, *, ctx_tiles, tn=1024):
    B, rows, K = y.shape
    N = w.shape[1]
    tm = ROW_TILE
    kind = lambda i: jnp.where(i < ctx_tiles, 0, 1)
    return pl.pallas_call(
        _mm_res_kernel,
        grid=(N // tn, B, rows // tm),
        in_specs=[pl.BlockSpec((1, tm, K), lambda j, b, i: (b, i, 0)),
                  pl.BlockSpec((K, tn), lambda j, b, i: (0, j)),
                  pl.BlockSpec((1, tm, tn), lambda j, b, i: (b, i, j)),
                  pl.BlockSpec((1, 1, 1, tn), lambda j, b, i: (b, kind(i), 0, j))],
        out_specs=pl.BlockSpec((1, tm, tn), lambda j, b, i: (b, i, j)),
        out_shape=jax.ShapeDtypeStruct((B, rows, N), f32),
        scratch_shapes=[pltpu.VMEM((K, tn), bf16)],
        compiler_params=_params("arbitrary", "arbitrary", "arbitrary"),
        name="mm_residual",
    )(y, w, x, gate)


def _rope_tile(y, tabs, half):
    c, s_lo, s_hi = tabs
    return y * c[...] + pltpu.roll(y, 128 - half, axis=1) * s_lo[...] + pltpu.roll(y, half, axis=1) * s_hi[...]


def _head_norm(x, g_ref):
    return x * lax.rsqrt(jnp.mean(x * x, axis=-1, keepdims=True) + NORM_EPS) * g_ref[...]


def _ones_column_tile(rows):
    lane = lax.broadcasted_iota(jnp.int32, (rows, 128), 1)
    return jnp.where(lane == 0, 1.0, 0.0).astype(bf16)


def _gqa_prep_kernel(p_ref, gq_ref, gk_ref, c_ref, s_lo_ref, s_hi_ref, q_ref, k_ref, v_ref):
    tabs = (c_ref, s_lo_ref, s_hi_ref)
    hd = HEAD_DIM
    q_scale = HEAD_DIM ** -0.5 * LOG2E
    for h in range(GQA_HEADS):
        x = p_ref[0, :, h * hd:(h + 1) * hd]
        q_ref[0, :, h * hd:(h + 1) * hd] = (_rope_tile(_head_norm(x, gq_ref), tabs, hd // 2) * q_scale).astype(bf16)
    k0 = GQA_HEADS * hd
    v0 = k0 + GQA_KV_HEADS * hd
    ones = _ones_column_tile(p_ref.shape[1])
    for h in range(GQA_KV_HEADS):
        x = p_ref[0, :, k0 + h * hd:k0 + (h + 1) * hd]
        k_ref[0, :, h * hd:(h + 1) * hd] = _rope_tile(_head_norm(x, gk_ref), tabs, hd // 2).astype(bf16)
        v_ref[0, :, 2 * h * hd:(2 * h + 1) * hd] = p_ref[0, :, v0 + h * hd:v0 + (h + 1) * hd].astype(bf16)
        v_ref[0, :, (2 * h + 1) * hd:(2 * h + 2) * hd] = ones


def gqa_prep(p_a, gq, gk, tabs):
    B, R, W = p_a.shape
    tm = ROW_TILE
    tab_spec = pl.BlockSpec((tm, 128), lambda b, r: (r, 0))
    gain_spec = pl.BlockSpec((1, HEAD_DIM), lambda b, r: (0, 0))
    row = lambda w: pl.BlockSpec((1, tm, w), lambda b, r: (b, r, 0))
    widths = (GQA_W, GQA_KV_HEADS * HEAD_DIM, GQA_KV_HEADS * 2 * HEAD_DIM)
    return pl.pallas_call(
        _gqa_prep_kernel,
        grid=(B, R // tm),
        in_specs=[row(W), gain_spec, gain_spec, tab_spec, tab_spec, tab_spec],
        out_specs=[row(w) for w in widths],
        out_shape=[jax.ShapeDtypeStruct((B, R, w), bf16) for w in widths],
        compiler_params=_params("arbitrary", "arbitrary"),
        name="gqa_prep",
    )(p_a, gq.reshape(1, HEAD_DIM), gk.reshape(1, HEAD_DIM), *tabs)


def _mla_prep_kernel(mq_ref, mkv_ref, ps_ref, c_ref, s_lo_ref, s_hi_ref, q_ref, k_ref, v_ref):
    tabs = (c_ref, s_lo_ref, s_hi_ref)
    half = MLA_ROPE // 2
    q_scale = (MLA_NOPE + MLA_ROPE) ** -0.5 * LOG2E
    k_rope = _rope_tile(ps_ref[0], tabs, half).astype(bf16)
    ones = _ones_column_tile(mq_ref.shape[1])
    P = MLA_QK_PAD
    for h in range(MLA_HEADS):
        lo, mid, hi = h * P, h * P + 128, (h + 1) * P
        q_ref[0, :, lo:mid] = (mq_ref[0, :, lo:mid] * q_scale).astype(bf16)
        q_ref[0, :, mid:hi] = (_rope_tile(mq_ref[0, :, mid:hi], tabs, half) * q_scale).astype(bf16)
        k_ref[0, :, lo:mid] = mkv_ref[0, :, lo:mid].astype(bf16)
        k_ref[0, :, mid:hi] = k_rope
        v_ref[0, :, lo:mid] = mkv_ref[0, :, mid:hi].astype(bf16)
        v_ref[0, :, mid:hi] = ones


def mla_prep(mq, mkv, p_s, tabs):
    B, R, W = mq.shape
    tm = ROW_TILE
    tab_spec = pl.BlockSpec((tm, 128), lambda b, r: (r, 0))
    row = lambda w: pl.BlockSpec((1, tm, w), lambda b, r: (b, r, 0))
    return pl.pallas_call(
        _mla_prep_kernel,
        grid=(B, R // tm),
        in_specs=[row(W), row(W), row(128), tab_spec, tab_spec, tab_spec],
        out_specs=[row(W)] * 3,
        out_shape=[jax.ShapeDtypeStruct((B, R, W), bf16)] * 3,
        compiler_params=_params("arbitrary", "arbitrary"),
        name="mla_prep",
    )(mq, mkv, p_s, *tabs)


ATT_ROWS = 256


def _attn_kernel(q_ref, k_ref, v_ref, o_ref, m_ref, acc_ref, *, tk, nk, group, splits, dq, dv):
    R = ATT_ROWS
    chains = [(g, r) for g in range(group) for r in range(splits)]
    m_ref[...] = jnp.full(m_ref.shape, -jnp.inf, f32)
    acc_ref[...] = jnp.zeros(acc_ref.shape, f32)

    def body(c, carry):
        off = pl.multiple_of(c * tk, tk)
        kc = k_ref[0, pl.ds(off, tk), :]
        vc = v_ref[0, pl.ds(off, tk), :]
        for n, (g, r) in enumerate(chains):
            q = q_ref[0, r * R:(r + 1) * R, g * dq:(g + 1) * dq]
            s = lax.dot_general(q, kc, (((1,), (1,)), ((), ())), preferred_element_type=f32)
            m_old = m_ref[n]
            m_new = jnp.maximum(m_old, jnp.max(s, axis=1, keepdims=True))
            p = jnp.exp2(s - m_new)
            alpha = jnp.exp2(m_old - m_new)
            acc_ref[n] = alpha * acc_ref[n] + jnp.dot(p.astype(bf16), vc, preferred_element_type=f32)
            m_ref[n] = m_new
        return carry

    lax.fori_loop(0, nk, body, 0)
    for n, (g, r) in enumerate(chains):
        acc = acc_ref[n]
        o_ref[0, r * R:(r + 1) * R, g * dv:(g + 1) * dv] = (acc[:, :dv] / acc[:, dv:dv + 1]).astype(o_ref.dtype)


def attention(q, k, v, *, heads, kv_heads, dq, dv, tq, tk, q_start=0, q_rows=None):
    B, Sq, _ = q.shape
    Sk = k.shape[1]
    q_rows = Sq - q_start if q_rows is None else q_rows
    group = heads // kv_heads
    splits = tq // ATT_ROWS
    assert q_rows % tq == 0 and q_start % tq == 0 and Sk % tk == 0 and tq % ATT_ROWS == 0
    first = q_start // tq
    chains = group * splits
    return pl.pallas_call(
        functools.partial(_attn_kernel, tk=tk, nk=Sk // tk, group=group, splits=splits, dq=dq, dv=dv),
        grid=(B, kv_heads, q_rows // tq),
        in_specs=[pl.BlockSpec((1, tq, group * dq), lambda b, h, i: (b, i + first, h)),
                  pl.BlockSpec((1, Sk, dq), lambda b, h, i: (b, 0, h)),
                  pl.BlockSpec((1, Sk, 2 * dv), lambda b, h, i: (b, 0, h))],
        out_specs=pl.BlockSpec((1, tq, group * dv), lambda b, h, i: (b, i, h)),
        out_shape=jax.ShapeDtypeStruct((B, q_rows, heads * dv), f32),
        scratch_shapes=[pltpu.VMEM((chains, ATT_ROWS, 1), f32),
                        pltpu.VMEM((chains, ATT_ROWS, 2 * dv), f32)],
        compiler_params=_params("arbitrary", "arbitrary", "arbitrary"),
        name="attention",
    )(q, k, v)


def _exact_dot_01(t01, x):
    h1 = x.astype(bf16)
    r1 = x - h1.astype(f32)
    h2 = r1.astype(bf16)
    h3 = (r1 - h2.astype(f32)).astype(bf16)
    d = lambda h: jnp.dot(t01, h, preferred_element_type=f32)
    return d(h1) + d(h2) + d(h3)


def _gla_kernel(q_ref, k_ref, v_ref, lr_ref, wa_ref, ba_ref, o_ref, state_ref, b_ref, *, rev):
    c = pl.program_id(1)
    C, SB = GLA_CHUNK, GLA_SUB
    nt = (((1,), (1,)), ((), ()))

    @pl.when(c == 0)
    def _():
        state_ref[...] = jnp.zeros_like(state_ref)

    row = lax.broadcasted_iota(jnp.int32, (C, C), 0)
    col = lax.broadcasted_iota(jnp.int32, (C, C), 1)
    tri = (row <= col) if rev else (row >= col)
    lr0 = GLA_LR_COL + int(rev) * GLA_GATE_RANK
    lr = lr_ref[0][:, lr0:lr0 + GLA_GATE_RANK]
    pre = jnp.dot(lr.astype(bf16), wa_ref[...], preferred_element_type=f32) + ba_ref[...]
    log_a = (jnp.minimum(pre, 0.0) - jnp.log1p(jnp.exp(-jnp.abs(pre)))) * (1.0 / GLA_TAU)
    b_ref[...] = _exact_dot_01(tri.astype(bf16), log_a)
    end = 0 if rev else C - 1
    krow = lax.broadcasted_iota(jnp.int32, (C, GLA_DK), 0)
    lane = lax.broadcasted_iota(jnp.int32, (SB, C), 1)
    trow = lax.broadcasted_iota(jnp.int32, (SB, C), 0)

    for h in range(GLA_HEADS):
        ks = slice(h * GLA_DK, (h + 1) * GLA_DK)
        vs = slice(h * GLA_DV, (h + 1) * GLA_DV)
        qh = q_ref[0, :, ks] * (GLA_DK ** -0.5)
        kh = k_ref[0, :, ks]
        vh = v_ref[0, :, vs].astype(bf16)
        bh = b_ref[:, ks]
        b_end = b_ref[end:end + 1, ks]
        st = state_ref[h]

        o_inter = lax.dot_general((qh * jnp.exp(bh)).astype(bf16), st.astype(bf16), nt,
                                  preferred_element_type=f32)

        blocks = []
        for i in range(C // SB):
            r0 = i * SB
            qi, bi = qh[r0:r0 + SB], bh[r0:r0 + SB]
            has_earlier = i < C // SB - 1 if rev else i > 0
            if has_earlier:
                first = r0 + SB - 1 if rev else r0
                b_first = b_ref[first:first + 1, ks]
                earlier = (krow >= r0 + SB) if rev else (krow < r0)
                k_dec = jnp.where(earlier, kh * jnp.exp(jnp.minimum(b_first - bh, 0.0)), 0.0)
                q_dec = qi * jnp.exp(jnp.minimum(bi - b_first, 0.0))
                blk = lax.dot_general(q_dec.astype(bf16), k_dec.astype(bf16), nt, preferred_element_type=f32)
            else:
                blk = jnp.zeros((SB, C), f32)
            diag = jnp.zeros((SB, C), f32)
            for u in range(SB):
                s = r0 + u
                x = qi * jnp.exp(jnp.minimum(bi - b_ref[s:s + 1, ks], 0.0)) * k_ref[0, s:s + 1, ks]
                diag = jnp.where(lane == s, jnp.sum(x, axis=1, keepdims=True), diag)
            causal = (lane >= trow + r0) if rev else (lane <= trow + r0)
            blocks.append(blk + jnp.where(causal, diag, 0.0))
        scores = jnp.concatenate(blocks, axis=0)
        o_intra = jnp.dot(scores.astype(bf16), vh, preferred_element_type=f32)
        o_ref[0, :, vs] = o_inter + o_intra

        k_end = (kh * jnp.exp(b_end - bh)).astype(bf16)
        state_ref[h] = st * jnp.exp(b_end) + lax.dot_general(
            vh, k_end, (((0,), (0,)), ((), ())), preferred_element_type=f32)


def gla_scans(p_l, p_s, wa, ba, *, ctx_len):
    B, R, _ = p_l.shape
    C = GLA_CHUNK
    n = R // C
    cc = ctx_len // C
    wide = GLA_HEADS * GLA_DK
    assert GLA_W == 2 * wide

    def scan(rev):
        chunk = (lambda c: jnp.where(c < cc, cc - 1 - c, n - 1 + cc - c)) if rev else (lambda c: c)
        d = int(rev)
        return pl.pallas_call(
            functools.partial(_gla_kernel, rev=rev),
            grid=(B, n),
            in_specs=[pl.BlockSpec((1, C, wide), lambda b, c: (b, chunk(c), 0)),
                      pl.BlockSpec((1, C, wide), lambda b, c: (b, chunk(c), 1)),
                      pl.BlockSpec((1, C, GLA_W), lambda b, c: (b, chunk(c), 1)),
                      pl.BlockSpec((1, C, 128), lambda b, c: (b, chunk(c), 0)),
                      pl.BlockSpec((GLA_GATE_RANK, wide), lambda b, c: (0, 0)),
                      pl.BlockSpec((1, wide), lambda b, c: (0, 0))],
            out_specs=pl.BlockSpec((1, C, GLA_W), lambda b, c: (b, chunk(c), 0)),
            out_shape=jax.ShapeDtypeStruct((B, R, GLA_W), f32),
            scratch_shapes=[pltpu.VMEM((GLA_HEADS, GLA_DV, GLA_DK), f32),
                            pltpu.VMEM((C, wide), f32)],
            compiler_params=_params("arbitrary", "arbitrary"),
            name="gla_scan_rev" if rev else "gla_scan_fwd",
        )(p_l, p_l, p_l, p_s, wa[d], ba[d])

    return scan(False), scan(True)


def _top16(s, ids):
    big = jnp.float32(1e9)
    vals, picks = [], []
    for _ in range(PEER_TOPK):
        m = jnp.max(s, axis=0, keepdims=True)
        pick = jnp.min(jnp.where(s == m, ids, big), axis=0, keepdims=True)
        vals.append(m)
        picks.append(pick)
        s = jnp.where(ids == pick, -jnp.inf, s)
    return jnp.concatenate(vals, axis=0), jnp.concatenate(picks, axis=0)


def _route_head(q, k1, k2):
    nt = (((1,), (1,)), ((), ()))
    s1 = lax.dot_general(k1, q[:, :PEER_HALF], nt, preferred_element_type=f32)
    s2 = lax.dot_general(k2, q[:, PEER_HALF:], nt, preferred_element_type=f32)
    key_id = lax.broadcasted_iota(jnp.int32, s1.shape, 0).astype(f32)
    v1, i1 = _top16(s1, key_id)
    v2, i2 = _top16(s2, key_id)
    half = PEER_TOPK // 2
    sub = lax.broadcasted_iota(jnp.int32, (PEER_TOPK, v1.shape[1]), 0).astype(f32)
    nb = lambda a: PEER_TOPK if a == 0 else half
    cand = jnp.concatenate([v1[a:a + 1] + v2[:nb(a)] for a in range(half)] + [v1[half:] + v2[0:1]], axis=0)
    cand_e = jnp.concatenate([i1[a:a + 1] * PEER_NKEYS + i2[:nb(a)] for a in range(half)]
                             + [i1[half:] * PEER_NKEYS + i2[0:1]], axis=0)
    pos = jnp.concatenate([a * PEER_TOPK + sub[:nb(a)] for a in range(half)]
                          + [(sub[:half] + half) * PEER_TOPK], axis=0)
    best, bpos = _top16(cand, pos)
    experts = [jnp.max(jnp.where(pos == bpos[r:r + 1], cand_e, -1.0), axis=0, keepdims=True)
               for r in range(PEER_TOPK)]
    e = jnp.exp(best - best[0:1])
    return jnp.concatenate(experts, axis=0).astype(jnp.int32), e / jnp.sum(e, axis=0, keepdims=True)


def _router_kernel(q_ref, k1_ref, k2_ref, idx_ref, g_ref):
    for h in range(ROUTE_HEADS_PER_STEP):
        q = q_ref[:, h * PEER_QDIM:(h + 1) * PEER_QDIM].astype(bf16)
        idx_ref[h], g_ref[h] = _route_head(q, k1_ref[h], k2_ref[h])


def peer_route(q, k1, k2):
    T = q.shape[0]
    tt = 128
    hs = ROUTE_HEADS_PER_STEP
    return pl.pallas_call(
        _router_kernel,
        grid=(T // tt, PEER_HEADS // hs),
        in_specs=[pl.BlockSpec((tt, hs * PEER_QDIM), lambda i, h: (i, h)),
                  pl.BlockSpec((hs, PEER_NKEYS, PEER_HALF), lambda i, h: (h, 0, 0)),
                  pl.BlockSpec((hs, PEER_NKEYS, PEER_HALF), lambda i, h: (h, 0, 0))],
        out_specs=[pl.BlockSpec((hs, PEER_TOPK, tt), lambda i, h: (h, 0, i)),
                   pl.BlockSpec((hs, PEER_TOPK, tt), lambda i, h: (h, 0, i))],
        out_shape=[jax.ShapeDtypeStruct((PEER_HEADS, PEER_TOPK, T), jnp.int32),
                   jax.ShapeDtypeStruct((PEER_HEADS, PEER_TOPK, T), f32)],
        compiler_params=_params("arbitrary", "arbitrary"),
        name="peer_route",
    )(q, k1, k2)


PEER_TT = 128
PEER_AHEAD = 3
PEER_SLOTS = PEER_AHEAD + 1
PEER_NEXT = 8
assert PEER_NEXT >= PEER_AHEAD and ((PEER_TT + PEER_NEXT) * PEER_SEL) % 1024 == 0
assert PEER_TT % PEER_SLOTS == 0


def _unpack_words(w):
    lo = pltpu.bitcast(w << 16, f32)
    hi = pltpu.bitcast(w & jnp.uint32(0xFFFF0000), f32)
    return lo, hi


def _sum_sublanes_8(parts):
    sub = lax.broadcasted_iota(jnp.int32, (8, 128), 0)
    dist = 4
    while len(parts) > 1:
        nxt = []
        half = len(parts) // 2
        for a in range(half):
            lo_rows = parts[a] + pltpu.roll(parts[a], 8 - dist, axis=0)
            hi_rows = parts[a + half] + pltpu.roll(parts[a + half], dist, axis=0)
            nxt.append(jnp.where((sub & dist) == 0, lo_rows, hi_rows))
        parts = nxt
        dist //= 2
    return parts[0]


def _peer_kernel(idx_ref, tab_ref, h_ref, g_ref, o_ref, *scratch):
    bufs, (sem, part_ref, w_ref) = scratch[:PEER_SLOTS], scratch[PEER_SLOTS:]
    NS = PEER_SEL
    step = pl.program_id(0)

    def issue(t, slot):
        for j in range(NS):
            e = idx_ref[t * NS + j]
            pltpu.make_async_copy(tab_ref.at[e], bufs[slot].at[:, :, j, :], sem.at[slot]).start(priority=j % 2)

    def wait(slot):
        pltpu.make_async_copy(bufs[slot], bufs[slot], sem.at[slot]).wait()

    @pl.when(step == 0)
    def _():
        for t in range(PEER_AHEAD):
            issue(t, t)

    lane = lax.broadcasted_iota(jnp.int32, (NS, PEER_TT), 1)

    def compute(t, slot):
        buf = bufs[slot]
        hrow = h_ref[t]
        hb = [jnp.broadcast_to(hrow[r:r + 1], (8, 128)) for r in range(16)]
        for jg in range(NS // 8):
            js = slice(jg * 8, (jg + 1) * 8)
            acc = None
            for s in range(8):
                lo, hi = _unpack_words(buf[0, s, js, :])
                term = lo * hb[2 * s] + hi * hb[2 * s + 1]
                acc = term if acc is None else acc + term
            part_ref[js, :] = acc
        act = jnp.sum(part_ref[...], axis=1, keepdims=True)
        gate = jnp.sum(jnp.where(lane == t, g_ref[...], 0.0), axis=1, keepdims=True)
        w = gate * (0.5 * act * (1.0 + lax.erf(act * 0.7071067811865476)))
        w_ref[...] = jnp.broadcast_to(w, (NS, 128))
        rows = []
        for s in range(8):
            acc_lo = acc_hi = None
            for jg in range(NS // 8):
                js = slice(jg * 8, (jg + 1) * 8)
                lo, hi = _unpack_words(buf[1, s, js, :])
                wv = w_ref[js, :]
                acc_lo = wv * lo if acc_lo is None else acc_lo + wv * lo
                acc_hi = wv * hi if acc_hi is None else acc_hi + wv * hi
            rows += [acc_lo, acc_hi]
        o_ref[t] = jnp.concatenate([_sum_sublanes_8(rows[0:8]), _sum_sublanes_8(rows[8:16])], axis=0)

    def group(tg, carry):
        for u in range(PEER_SLOTS):
            t = tg * PEER_SLOTS + u
            wait(u)
            issue(t + PEER_AHEAD, (u + PEER_AHEAD) % PEER_SLOTS)
            compute(t, u)
        return carry

    lax.fori_loop(0, PEER_TT // PEER_SLOTS, group, 0)

    @pl.when(step == pl.num_programs(0) - 1)
    def _():
        for t in range(PEER_AHEAD):
            wait(t)


def peer_experts(idx, table, h, gates):
    T = h.shape[0]
    tt = PEER_TT
    tiles = idx.reshape(T // tt, tt, PEER_SEL)
    idx_ext = jnp.concatenate([tiles, jnp.roll(tiles, -1, axis=0)[:, :PEER_NEXT]], axis=1).reshape(-1)
    return pl.pallas_call(
        _peer_kernel,
        grid=(T // tt,),
        in_specs=[pl.BlockSpec(((tt + PEER_NEXT) * PEER_SEL,), lambda i: (i,), memory_space=pltpu.SMEM),
                  pl.BlockSpec(memory_space=pl.ANY),
                  pl.BlockSpec((tt, 16, 128), lambda i: (i, 0, 0)),
                  pl.BlockSpec((PEER_SEL, tt), lambda i: (0, i))],
        out_specs=pl.BlockSpec((tt, 16, 128), lambda i: (i, 0, 0)),
        out_shape=jax.ShapeDtypeStruct((T, 16, 128), f32),
        scratch_shapes=[pltpu.VMEM((2, 8, PEER_SEL, 128), jnp.uint32)] * PEER_SLOTS + [
                        pltpu.SemaphoreType.DMA((PEER_SLOTS,)),
                        pltpu.VMEM((PEER_SEL, 128), f32),
                        pltpu.VMEM((PEER_SEL, 128), f32)],
        compiler_params=_params("arbitrary"),
        name="peer_experts",
    )(idx_ext, table, h, gates)


def _pack_rows(tab):
    E = tab.shape[0]
    bits = lax.bitcast_convert_type(tab.astype(bf16), jnp.uint16).astype(jnp.uint32).reshape(E, 8, 2, 128)
    return bits[:, :, 0, :] | (bits[:, :, 1, :] << 16)


def peer_ffn(h2, h2_tiles, wq, k1, k2, table):
    T = h2.shape[0]
    q = mm(h2, wq, tn=1024)
    idx, gates = peer_route(q, k1.astype(bf16), k2.astype(bf16))
    idx_tok = idx.reshape(PEER_SEL, T).T
    return peer_experts(idx_tok, table, h2_tiles, gates.reshape(PEER_SEL, T))


def _rope_tables(n_ctx, seq, dim):
    quarter = dim // 4
    t = jnp.arange(seq, dtype=jnp.int32)
    inv_freq = ROPE_THETA ** (-jnp.arange(quarter, dtype=f32) / quarter)
    ang = jnp.concatenate([(t // GRID_W)[:, None].astype(f32) * inv_freq,
                           (t % GRID_W)[:, None].astype(f32) * inv_freq], axis=-1)
    cos = jnp.concatenate([jnp.ones((n_ctx, dim // 2), f32), jnp.cos(ang)], axis=0)
    sin = jnp.concatenate([jnp.zeros((n_ctx, dim // 2), f32), jnp.sin(ang)], axis=0)
    return cos, sin


def _rope_tile_tables(cos, sin):
    R, half = cos.shape
    z = jnp.zeros((R, 128 - 2 * half), f32)
    zh = jnp.zeros((R, half), f32)
    return (jnp.concatenate([cos, cos, z], axis=1),
            jnp.concatenate([-sin, zh, z], axis=1),
            jnp.concatenate([zh, sin, z], axis=1))


def _ada(vec, w, b):
    n = vec.shape[0]
    a = jnp.zeros((8, vec.shape[1]), f32).at[:n].set(jax.nn.silu(vec))
    return mm(a, w, tm=8, tn=1024)[:n] + b


def kernel(x, c, ctx, c_ctx, w_mod, b_mod, norm1_g, w_in, gqa_qn_g, gqa_kn_g, gla_wa2_f, gla_ba_f, gla_wa2_b, gla_ba_b, gla_on_g, mla_qn_g, mla_wuq, mla_kvn_g, mla_wukv, w_br_gqa, w_br_gla, w_br_mla, w_out, norm2_g, peer_wq, peer_k1, peer_k2, peer_u, peer_v, final_g):
    B, S, D = x.shape
    Sc = ctx.shape[1]
    depth = w_in.shape[0]
    R = Sc + S
    ctx_tiles = Sc // ROW_TILE
    tabs_a = _rope_tile_tables(*_rope_tables(Sc, S, HEAD_DIM))
    tabs_m = _rope_tile_tables(*_rope_tables(Sc, S, MLA_ROPE))
    xs = jnp.concatenate([ctx, x], axis=1)

    mods = []
    for i in range(depth):
        m = _ada(jnp.concatenate([c_ctx[None, :], c], axis=0), w_mod[i], b_mod[i])
        m = jnp.stack([jnp.broadcast_to(m[0], (B, 6 * D)), m[1:]], axis=1)
        mods.append([t[:, :, None, :] for t in jnp.split(m, 6, axis=-1)])

    h1, = norm_mod(xs, norm1_g[0], mods[0][1], mods[0][0], ctx_tiles=ctx_tiles, out_dtype=bf16)
    for i in range(depth):
        need_ctx = i < depth - 1
        sh1, sc1, g1, sh2, sc2, g2 = mods[i]

        h1 = h1.reshape(B * R, D)
        w = w_in[i]
        p_a = mm(h1, w[:, 0:1536], tn=768).reshape(B, R, 1536)
        p_l = mm(h1, w[:, 1536:4608], tn=1024).reshape(B, R, 3072)
        p_m = mm(h1, w[:, 4640:5664], tn=1024).reshape(B, R, 1024)
        w_small = jnp.concatenate([w[:, 5664:5728], w[:, 4608:4640], jnp.zeros((D, 32), f32)], axis=1)
        p_s = mm(h1, w_small, tn=128).reshape(B, R, 128)
        p_g = mm(h1, w[:, 5728:11872], tn=1024, out_dtype=bf16).reshape(B, R, 3 * D)

        aq, ak, av = gqa_prep(p_a, gqa_qn_g[i], gqa_kn_g[i], tabs_a)
        gqa = functools.partial(attention, heads=GQA_HEADS, kv_heads=GQA_KV_HEADS, dq=HEAD_DIM, dv=HEAD_DIM)
        o_a = gqa(aq, ak, av, tq=ATT_ROWS, tk=768, q_start=Sc)
        if need_ctx:
            o_a = jnp.concatenate([gqa(aq, ak[:, :Sc], av[:, :Sc], tq=Sc, tk=Sc, q_rows=Sc), o_a], axis=1)

        wuq = jnp.pad(mla_wuq[i].reshape(MLA_Q_RANK, MLA_HEADS, MLA_NOPE + MLA_ROPE),
                      ((0, 0), (0, 0), (0, MLA_QK_PAD - MLA_NOPE - MLA_ROPE))).reshape(MLA_Q_RANK, MLA_HEADS * MLA_QK_PAD)
        p_m2 = p_m.reshape(B * R, MLA_Q_RANK + MLA_KV_RANK)
        mq = mm_rms(p_m2, 0, mla_qn_g[i], wuq, tn=1024).reshape(B, R, MLA_HEADS * MLA_QK_PAD)
        mkv = mm_rms(p_m2, 1, mla_kvn_g[i], mla_wukv[i], tn=1024).reshape(B, R, MLA_HEADS * (MLA_NOPE + MLA_V))
        m_q, m_k, m_v = mla_prep(mq, mkv, p_s, tabs_m)
        mla = functools.partial(attention, heads=MLA_HEADS, kv_heads=MLA_HEADS, dq=MLA_QK_PAD, dv=MLA_V)
        o_m = mla(m_q[:, Sc:], m_k, m_v, tq=4 * ATT_ROWS, tk=768)
        if need_ctx:
            o_m = jnp.concatenate([mla(m_q, m_k[:, :Sc], m_v[:, :Sc], tq=Sc, tk=Sc, q_rows=Sc), o_m], axis=1)

        wa = jnp.stack([gla_wa2_f[i], gla_wa2_b[i]], axis=0).astype(bf16)
        ba = jnp.stack([gla_ba_f[i], gla_ba_b[i]], axis=0)[:, None, :]
        o_f, o_b = gla_scans(p_l, p_s, wa, ba, ctx_len=Sc)

        n_ctx = Sc if need_ctx else 0
        if not need_ctx:
            xs = xs[:, Sc:]
        y = merge_branches(o_a, o_f, o_b, p_l, o_m, p_g, w_br_gqa[i], w_br_gla[i], w_br_mla[i], gla_on_g[i],
                           row0=Sc - n_ctx)
        xs = mm_residual(y, w_out[i], xs, g1, ctx_tiles=n_ctx // ROW_TILE)
        rows = xs.shape[1]

        h2, h2_tiles = norm_mod(xs, norm2_g[i], sc2, sh2, ctx_tiles=n_ctx // ROW_TILE, out_dtype=bf16, emit_tiled=True)
        table = jnp.stack([_pack_rows(peer_u[i]), _pack_rows(peer_v[i])], axis=1)
        ff = peer_ffn(h2.reshape(B * rows, D), h2_tiles.reshape(B * rows, FEAT_ROWS, 128),
                      peer_wq[i], peer_k1[i], peer_k2[i], table).reshape(B * rows * FEAT_ROWS, 128)
        if need_ctx:
            xs, h1 = norm_mod(xs, norm1_g[i + 1], mods[i + 1][1], mods[i + 1][0], ctx_tiles=ctx_tiles, out_dtype=bf16,
                              res=ff, res_gate=g2, emit_x=True)
        else:
            zero = jnp.zeros((B, 2, 1, D), f32)
            out, = norm_mod(xs, final_g, zero, zero, ctx_tiles=0, out_dtype=f32, res=ff, res_gate=g2)
    return out
```

```python
import functools

import jax
import jax.numpy as jnp
from jax import lax
from jax.experimental import pallas as pl
from jax.experimental.pallas import tpu as pltpu

f32 = jnp.float32
bf16 = jnp.bfloat16

D_MODEL = 2048
GRID_W = 64
ROPE_THETA = 10000.0
NORM_EPS = 1e-6
LOG2E = 1.4426950408889634

GQA_HEADS = 8
GQA_KV_HEADS = 2
HEAD_DIM = 128
GLA_HEADS = 4
GLA_DK = 128
GLA_DV = 256
GLA_GATE_RANK = 16
GLA_TAU = 16.0
GLA_CHUNK = 64
GLA_SUB = 16
GLA_LR_COL = 64
MLA_HEADS = 8
MLA_Q_RANK = 512
MLA_KV_RANK = 512
MLA_NOPE = 128
MLA_ROPE = 64
MLA_V = 128
MLA_QK_PAD = 256
GQA_W = GQA_HEADS * HEAD_DIM
GLA_W = GLA_HEADS * GLA_DV
MLA_W = MLA_HEADS * MLA_V
PEER_HEADS = 8
PEER_NKEYS = 128
PEER_QDIM = 256
PEER_HALF = PEER_QDIM // 2
PEER_TOPK = 16
PEER_SEL = PEER_HEADS * PEER_TOPK
ROUTE_HEADS_PER_STEP = 2

VMEM_LIMIT = 48 * 1024 * 1024
ROW_TILE = 256


def _params(*sem):
    return pltpu.CompilerParams(dimension_semantics=sem, vmem_limit_bytes=VMEM_LIMIT)


def _mm_kernel(a_ref, b_ref, o_ref, bq_ref):
    @pl.when(pl.program_id(1) == 0)
    def _():
        bq_ref[...] = b_ref[...].astype(bf16)

    o_ref[...] = jnp.dot(a_ref[...].astype(bf16), bq_ref[...],
                         preferred_element_type=f32).astype(o_ref.dtype)


def mm(a, b, *, tm=512, tn=None, out_dtype=f32):
    M, K = a.shape
    N = b.shape[1]
    tn = N if tn is None else tn
    tm = min(tm, M)
    assert M % tm == 0 and N % tn == 0, (M, N, tm, tn)
    return pl.pallas_call(
        _mm_kernel,
        grid=(N // tn, M // tm),
        in_specs=[pl.BlockSpec((tm, K), lambda j, i: (i, 0)),
                  pl.BlockSpec((K, tn), lambda j, i: (0, j))],
        out_specs=pl.BlockSpec((tm, tn), lambda j, i: (i, j)),
        out_shape=jax.ShapeDtypeStruct((M, N), out_dtype),
        scratch_shapes=[pltpu.VMEM((K, tn), bf16)],
        compiler_params=_params("arbitrary", "arbitrary"),
        name="mm",
    )(a, b)


def _mm_rms_kernel(a_ref, g_ref, b_ref, o_ref, bq_ref):
    @pl.when(pl.program_id(1) == 0)
    def _():
        bq_ref[...] = b_ref[...].astype(bf16)

    a = a_ref[...]
    a = a * lax.rsqrt(jnp.mean(a * a, axis=-1, keepdims=True) + NORM_EPS) * g_ref[...]
    o_ref[...] = jnp.dot(a.astype(bf16), bq_ref[...], preferred_element_type=f32).astype(o_ref.dtype)


def mm_rms(a, a_col, g, b, *, tm=512, tn=None, out_dtype=f32):
    M = a.shape[0]
    K, N = b.shape
    tn = N if tn is None else tn
    assert M % tm == 0 and N % tn == 0
    return pl.pallas_call(
        _mm_rms_kernel,
        grid=(N // tn, M // tm),
        in_specs=[pl.BlockSpec((tm, K), lambda j, i: (i, a_col)),
                  pl.BlockSpec((1, K), lambda j, i: (0, 0)),
                  pl.BlockSpec((K, tn), lambda j, i: (0, j))],
        out_specs=pl.BlockSpec((tm, tn), lambda j, i: (i, j)),
        out_shape=jax.ShapeDtypeStruct((M, N), out_dtype),
        scratch_shapes=[pltpu.VMEM((K, tn), bf16)],
        compiler_params=_params("arbitrary", "arbitrary"),
        name="mm_rms",
    )(a, g.reshape(1, K), b)


FEAT_ROWS = D_MODEL // 128


def _norm_kernel(*refs, has_res, emit_x, emit_tiled):
    refs = list(refs)
    x_ref = refs.pop(0)
    x = x_ref[0]
    if has_res:
        ff_ref, gate_ref = refs.pop(0), refs.pop(0)
        ff = jnp.concatenate([ff_ref[pl.ds(r, ROW_TILE, stride=FEAT_ROWS), :] for r in range(FEAT_ROWS)], axis=1)
        x = x + gate_ref[0, 0] * ff
    g_ref, sc_ref, sh_ref = refs.pop(0), refs.pop(0), refs.pop(0)
    if emit_x:
        refs.pop(0)[0] = x
    y = x * lax.rsqrt(jnp.mean(x * x, axis=-1, keepdims=True) + NORM_EPS) * g_ref[...]
    h = y * (1.0 + sc_ref[0, 0]) + sh_ref[0, 0]
    h_ref = refs.pop(0)
    h_ref[0] = h.astype(h_ref.dtype)
    if emit_tiled:
        ht_ref = refs.pop(0)
        for r in range(FEAT_ROWS):
            ht_ref[pl.ds(r, ROW_TILE, stride=FEAT_ROWS), :] = h[:, r * 128:(r + 1) * 128]


def norm_mod(x, g, sc, sh, *, ctx_tiles, out_dtype, res=None, res_gate=None, emit_x=False, emit_tiled=False):
    B, R, D = x.shape
    nr = R // ROW_TILE
    kind = lambda r: jnp.where(r < ctx_tiles, 0, 1)
    row_spec = pl.BlockSpec((1, ROW_TILE, D), lambda b, r: (b, r, 0))
    mod_spec = pl.BlockSpec((1, 1, 1, D), lambda b, r: (b, kind(r), 0, 0))
    tile_spec = pl.BlockSpec((ROW_TILE * FEAT_ROWS, 128), lambda b, r: (b * nr + r, 0))
    args, in_specs = [x], [row_spec]
    if res is not None:
        args += [res, res_gate]
        in_specs += [tile_spec, mod_spec]
    args += [g.reshape(1, D), sc, sh]
    in_specs += [pl.BlockSpec((1, D), lambda b, r: (0, 0)), mod_spec, mod_spec]
    out_specs, out_shape = [], []
    if emit_x:
        out_specs.append(row_spec)
        out_shape.append(jax.ShapeDtypeStruct((B, R, D), f32))
    out_specs.append(row_spec)
    out_shape.append(jax.ShapeDtypeStruct((B, R, D), out_dtype))
    if emit_tiled:
        out_specs.append(tile_spec)
        out_shape.append(jax.ShapeDtypeStruct((B * R * FEAT_ROWS, 128), f32))
    return pl.pallas_call(
        functools.partial(_norm_kernel, has_res=res is not None, emit_x=emit_x, emit_tiled=emit_tiled),
        grid=(B, nr),
        in_specs=in_specs,
        out_specs=out_specs,
        out_shape=out_shape,
        compiler_params=_params("arbitrary", "arbitrary"),
        name="norm_mod",
    )(*args)


MERGE_TN = 512


def _merge_kernel(oa_ref, of_ref, ob_ref, lg_ref, om_ref, ga_ref, gl_ref, gm_ref, wa_ref, wl_ref, wm_ref, gn_ref,
                  o_ref, wa_q, wl_q, wm_q):
    @pl.when((pl.program_id(1) == 0) & (pl.program_id(2) == 0))
    def _():
        wa_q[...] = wa_ref[...].astype(bf16)
        wl_q[...] = wl_ref[...].astype(bf16)
        wm_q[...] = wm_ref[...].astype(bf16)

    o = of_ref[0] + ob_ref[0]
    lg = lg_ref[0]
    heads = []
    for h in range(GLA_HEADS):
        oh = o[:, h * GLA_DV:(h + 1) * GLA_DV]
        heads.append(oh * lax.rsqrt(jnp.mean(oh * oh, axis=-1, keepdims=True) + NORM_EPS) * gn_ref[...])
    o_l = jnp.concatenate(heads, axis=1) * (lg * jax.nn.sigmoid(lg))
    dot = lambda a, w: jnp.dot(a.astype(bf16), w[...], preferred_element_type=f32)
    sig = lambda r: jax.nn.sigmoid(r[0].astype(f32))
    y = (sig(ga_ref) * dot(oa_ref[0], wa_q) + sig(gl_ref) * dot(o_l, wl_q) + sig(gm_ref) * dot(om_ref[0], wm_q))
    o_ref[0] = y.astype(o_ref.dtype)


def merge_branches(o_a, o_f, o_b, p_l, o_m, p_g, wa, wl, wm, gn, *, row0, om_row0=0):
    B, rows, _ = o_a.shape
    D = wa.shape[1]
    tm, tn = ROW_TILE, MERGE_TN
    nb = D // tn
    r0, rm = row0 // tm, om_row0 // tm
    own = lambda w, off=0: pl.BlockSpec((1, tm, w), lambda j, b, i: (b, i + off, 0))
    full = lambda w, cb: pl.BlockSpec((1, tm, w), lambda j, b, i: (b, i + r0, cb))
    gate = lambda k: pl.BlockSpec((1, tm, tn), lambda j, b, i: (b, i + r0, k * nb + j))
    wspec = pl.BlockSpec((GQA_W, tn), lambda j, b, i: (0, j))
    return pl.pallas_call(
        _merge_kernel,
        grid=(nb, B, rows // tm),
        in_specs=[own(GQA_W), full(GLA_W, 0), full(GLA_W, 0), full(GLA_W, 2), own(MLA_W, rm),
                  gate(0), gate(1), gate(2), wspec, wspec, wspec,
                  pl.BlockSpec((1, GLA_DV), lambda j, b, i: (0, 0))],
        out_specs=pl.BlockSpec((1, tm, tn), lambda j, b, i: (b, i, j)),
        out_shape=jax.ShapeDtypeStruct((B, rows, D), bf16),
        scratch_shapes=[pltpu.VMEM((GQA_W, tn), bf16)] * 3,
        compiler_params=_params("arbitrary", "arbitrary", "arbitrary"),
        name="merge_branches",
    )(o_a, o_f, o_b, p_l, o_m, p_g, p_g, p_g, wa, wl, wm, gn.reshape(1, GLA_DV))


def _mm_res_kernel(y_ref, w_ref, x_ref, g_ref, o_ref, wq_ref):
    @pl.when((pl.program_id(1) == 0) & (pl.program_id(2) == 0))
    def _():
        wq_ref[...] = w_ref[...].astype(bf16)

    o_ref[0] = x_ref[0] + g_ref[0, 0] * jnp.dot(y_ref[0], wq_ref[...], preferred_element_type=f32)


def mm_residual(y, w, x, gate, *, ctx_tiles, tn=1024):
    B, rows, K = y.shape
    N = w.shape[1]
    tm = ROW_TILE
    kind = lambda i: jnp.where(i < ctx_tiles, 0, 1)
    return pl.pallas_call(
        _mm_res_kernel,
        grid=(N // tn, B, rows // tm),
        in_specs=[pl.BlockSpec((1, tm, K), lambda j, b, i: (b, i, 0)),
                  pl.BlockSpec((K, tn), lambda j, b, i: (0, j)),
                  pl.BlockSpec((1, tm, tn), lambda j, b, i: (b, i, j)),
                  pl.BlockSpec((1, 1, 1, tn), lambda j, b, i: (b, kind(i), 0, j))],
        out_specs=pl.BlockSpec((1, tm, tn), lambda j, b, i: (b, i, j)),
        out_shape=jax.ShapeDtypeStruct((B, rows, N), f32),
        scratch_shapes=[pltpu.VMEM((K, tn), bf16)],
        compiler_params=_params("arbitrary", "arbitrary", "arbitrary"),
        name="mm_residual",
    )(y, w, x, gate)


def _rope_tile(y, tabs, half):
    c, s_lo, s_hi = tabs
    return y * c[...] + pltpu.roll(y, 128 - half, axis=1) * s_lo[...] + pltpu.roll(y, half, axis=1) * s_hi[...]


def _head_norm(x, g_ref):
    return x * lax.rsqrt(jnp.mean(x * x, axis=-1, keepdims=True) + NORM_EPS) * g_ref[...]


def _ones_column_tile(rows):
    lane = lax.broadcasted_iota(jnp.int32, (rows, 128), 1)
    return jnp.where(lane == 0, 1.0, 0.0).astype(bf16)


def _gqa_prep_kernel(p_ref, gq_ref, gk_ref, c_ref, s_lo_ref, s_hi_ref, q_ref, k_ref, v_ref):
    tabs = (c_ref, s_lo_ref, s_hi_ref)
    hd = HEAD_DIM
    q_scale = HEAD_DIM ** -0.5 * LOG2E
    for h in range(GQA_HEADS):
        x = p_ref[0, :, h * hd:(h + 1) * hd]
        q_ref[0, :, h * hd:(h + 1) * hd] = (_rope_tile(_head_norm(x, gq_ref), tabs, hd // 2) * q_scale).astype(bf16)
    k0 = GQA_HEADS * hd
    v0 = k0 + GQA_KV_HEADS * hd
    ones = _ones_column_tile(p_ref.shape[1])
    for h in range(GQA_KV_HEADS):
        x = p_ref[0, :, k0 + h * hd:k0 + (h + 1) * hd]
        k_ref[0, :, h * hd:(h + 1) * hd] = _rope_tile(_head_norm(x, gk_ref), tabs, hd // 2).astype(bf16)
        v_ref[0, :, 2 * h * hd:(2 * h + 1) * hd] = p_ref[0, :, v0 + h * hd:v0 + (h + 1) * hd].astype(bf16)
        v_ref[0, :, (2 * h + 1) * hd:(2 * h + 2) * hd] = ones


def gqa_prep(p_a, gq, gk, tabs):
    B, R, W = p_a.shape
    tm = ROW_TILE
    tab_spec = pl.BlockSpec((tm, 128), lambda b, r: (r, 0))
    gain_spec = pl.BlockSpec((1, HEAD_DIM), lambda b, r: (0, 0))
    row = lambda w: pl.BlockSpec((1, tm, w), lambda b, r: (b, r, 0))
    widths = (GQA_W, GQA_KV_HEADS * HEAD_DIM, GQA_KV_HEADS * 2 * HEAD_DIM)
    return pl.pallas_call(
        _gqa_prep_kernel,
        grid=(B, R // tm),
        in_specs=[row(W), gain_spec, gain_spec, tab_spec, tab_spec, tab_spec],
        out_specs=[row(w) for w in widths],
        out_shape=[jax.ShapeDtypeStruct((B, R, w), bf16) for w in widths],
        compiler_params=_params("arbitrary", "arbitrary"),
        name="gqa_prep",
    )(p_a, gq.reshape(1, HEAD_DIM), gk.reshape(1, HEAD_DIM), *tabs)


def _mla_prep_kernel(mq_ref, mkv_ref, ps_ref, c_ref, s_lo_ref, s_hi_ref, q_ref, k_ref, v_ref):
    tabs = (c_ref, s_lo_ref, s_hi_ref)
    half = MLA_ROPE // 2
    q_scale = (MLA_NOPE + MLA_ROPE) ** -0.5 * LOG2E
    k_rope = _rope_tile(ps_ref[0], tabs, half).astype(bf16)
    ones = _ones_column_tile(mq_ref.shape[1])
    P = MLA_QK_PAD
    for h in range(MLA_HEADS):
        lo, mid, hi = h * P, h * P + 128, (h + 1) * P
        q_ref[0, :, lo:mid] = (mq_ref[0, :, lo:mid] * q_scale).astype(bf16)
        q_ref[0, :, mid:hi] = (_rope_tile(mq_ref[0, :, mid:hi], tabs, half) * q_scale).astype(bf16)
        k_ref[0, :, lo:mid] = mkv_ref[0, :, lo:mid].astype(bf16)
        k_ref[0, :, mid:hi] = k_rope
        v_ref[0, :, lo:mid] = mkv_ref[0, :, mid:hi].astype(bf16)
        v_ref[0, :, mid:hi] = ones


def mla_prep(mq, mkv, p_s, tabs):
    B, R, W = mq.shape
    tm = ROW_TILE
    tab_spec = pl.BlockSpec((tm, 128), lambda b, r: (r, 0))
    row = lambda w: pl.BlockSpec((1, tm, w), lambda b, r: (b, r, 0))
    return pl.pallas_call(
        _mla_prep_kernel,
        grid=(B, R // tm),
        in_specs=[row(W), row(W), row(128), tab_spec, tab_spec, tab_spec],
        out_specs=[row(W)] * 3,
        out_shape=[jax.ShapeDtypeStruct((B, R, W), bf16)] * 3,
        compiler_params=_params("arbitrary", "arbitrary"),
        name="mla_prep",
    )(mq, mkv, p_s, *tabs)


ATT_ROWS = 256


def _attn_kernel(q_ref, k_ref, v_ref, o_ref, m_ref, acc_ref, *, tk, nk, group, splits, dq, dv):
    R = ATT_ROWS
    chains = [(g, r) for g in range(group) for r in range(splits)]
    m_ref[...] = jnp.full(m_ref.shape, -jnp.inf, f32)
    acc_ref[...] = jnp.zeros(acc_ref.shape, f32)

    def body(c, carry):
        off = pl.multiple_of(c * tk, tk)
        kc = k_ref[0, pl.ds(off, tk), :]
        vc = v_ref[0, pl.ds(off, tk), :]
        for n, (g, r) in enumerate(chains):
            q = q_ref[0, r * R:(r + 1) * R, g * dq:(g + 1) * dq]
            s = lax.dot_general(q, kc, (((1,), (1,)), ((), ())), preferred_element_type=f32)
            m_old = m_ref[n]
            m_new = jnp.maximum(m_old, jnp.max(s, axis=1, keepdims=True))
            p = jnp.exp2(s - m_new)
            alpha = jnp.exp2(m_old - m_new)
            acc_ref[n] = alpha * acc_ref[n] + jnp.dot(p.astype(bf16), vc, preferred_element_type=f32)
            m_ref[n] = m_new
        return carry

    lax.fori_loop(0, nk, body, 0)
    for n, (g, r) in enumerate(chains):
        acc = acc_ref[n]
        o_ref[0, r * R:(r + 1) * R, g * dv:(g + 1) * dv] = (acc[:, :dv] / acc[:, dv:dv + 1]).astype(o_ref.dtype)


def _attn_fill_kernel(kern, q_ref, k_ref, v_ref, fill_ref, o_ref, *scratch):
    del fill_ref
    kern(q_ref, k_ref, v_ref, o_ref, *scratch)


def attention(q, k, v, *, heads, kv_heads, dq, dv, tq, tk, q_start=0, q_rows=None, out_rows=None, out_start=0,
              fill=None):
    B, Sq, _ = q.shape
    Sk = k.shape[1]
    q_rows = Sq - q_start if q_rows is None else q_rows
    out_rows = q_rows if out_rows is None else out_rows
    group = heads // kv_heads
    splits = tq // ATT_ROWS
    assert q_rows % tq == 0 and q_start % tq == 0 and out_start % tq == 0 and Sk % tk == 0 and tq % ATT_ROWS == 0
    first, out_first = q_start // tq, out_start // tq
    chains = group * splits
    in_specs = [pl.BlockSpec((1, tq, group * dq), lambda b, h, i: (b, i + first, h)),
                pl.BlockSpec((1, Sk, dq), lambda b, h, i: (b, 0, h)),
                pl.BlockSpec((1, Sk, 2 * dv), lambda b, h, i: (b, 0, h))]
    args = [q, k, v]
    kern = functools.partial(_attn_kernel, tk=tk, nk=Sk // tk, group=group, splits=splits, dq=dq, dv=dv)
    if fill is not None:
        in_specs.append(pl.BlockSpec(memory_space=pl.ANY))
        args.append(fill)
        kern = functools.partial(_attn_fill_kernel, kern)
    return pl.pallas_call(
        kern,
        grid=(B, kv_heads, q_rows // tq),
        in_specs=in_specs,
        out_specs=pl.BlockSpec((1, tq, group * dv), lambda b, h, i: (b, i + out_first, h)),
        out_shape=jax.ShapeDtypeStruct((B, out_rows, heads * dv), f32),
        scratch_shapes=[pltpu.VMEM((chains, ATT_ROWS, 1), f32),
                        pltpu.VMEM((chains, ATT_ROWS, 2 * dv), f32)],
        input_output_aliases={3: 0} if fill is not None else {},
        compiler_params=_params("arbitrary", "arbitrary", "arbitrary"),
        name="attention",
    )(*args)


def _exact_dot_01(t01, x):
    h1 = x.astype(bf16)
    r1 = x - h1.astype(f32)
    h2 = r1.astype(bf16)
    h3 = (r1 - h2.astype(f32)).astype(bf16)
    d = lambda h: jnp.dot(t01, h, preferred_element_type=f32)
    return d(h1) + d(h2) + d(h3)


def _gla_kernel(q_ref, k_ref, v_ref, lr_ref, wa_ref, ba_ref, o_ref, state_ref, b_ref, *, rev):
    c = pl.program_id(1)
    C, SB = GLA_CHUNK, GLA_SUB
    nt = (((1,), (1,)), ((), ()))

    @pl.when(c == 0)
    def _():
        state_ref[...] = jnp.zeros_like(state_ref)

    row = lax.broadcasted_iota(jnp.int32, (C, C), 0)
    col = lax.broadcasted_iota(jnp.int32, (C, C), 1)
    tri = (row <= col) if rev else (row >= col)
    lr0 = GLA_LR_COL + int(rev) * GLA_GATE_RANK
    lr = lr_ref[0][:, lr0:lr0 + GLA_GATE_RANK]
    pre = jnp.dot(lr.astype(bf16), wa_ref[...], preferred_element_type=f32) + ba_ref[...]
    log_a = (jnp.minimum(pre, 0.0) - jnp.log1p(jnp.exp(-jnp.abs(pre)))) * (1.0 / GLA_TAU)
    b_ref[...] = _exact_dot_01(tri.astype(bf16), log_a)
    end = 0 if rev else C - 1
    krow = lax.broadcasted_iota(jnp.int32, (C, GLA_DK), 0)
    lane = lax.broadcasted_iota(jnp.int32, (SB, C), 1)
    trow = lax.broadcasted_iota(jnp.int32, (SB, C), 0)

    for h in range(GLA_HEADS):
        ks = slice(h * GLA_DK, (h + 1) * GLA_DK)
        vs = slice(h * GLA_DV, (h + 1) * GLA_DV)
        qh = q_ref[0, :, ks] * (GLA_DK ** -0.5)
        kh = k_ref[0, :, ks]
        vh = v_ref[0, :, vs].astype(bf16)
        bh = b_ref[:, ks]
        b_end = b_ref[end:end + 1, ks]
        st = state_ref[h]

        o_inter = lax.dot_general((qh * jnp.exp(bh)).astype(bf16), st.astype(bf16), nt,
                                  preferred_element_type=f32)

        blocks = []
        for i in range(C // SB):
            r0 = i * SB
            qi, bi = qh[r0:r0 + SB], bh[r0:r0 + SB]
            has_earlier = i < C // SB - 1 if rev else i > 0
            if has_earlier:
                first = r0 + SB - 1 if rev else r0
                b_first = b_ref[first:first + 1, ks]
                earlier = (krow >= r0 + SB) if rev else (krow < r0)
                k_dec = jnp.where(earlier, kh * jnp.exp(jnp.minimum(b_first - bh, 0.0)), 0.0)
                q_dec = qi * jnp.exp(jnp.minimum(bi - b_first, 0.0))
                blk = lax.dot_general(q_dec.astype(bf16), k_dec.astype(bf16), nt, preferred_element_type=f32)
            else:
                blk = jnp.zeros((SB, C), f32)
            diag = jnp.zeros((SB, C), f32)
            for u in range(SB):
                s = r0 + u
                x = qi * jnp.exp(jnp.minimum(bi - b_ref[s:s + 1, ks], 0.0)) * k_ref[0, s:s + 1, ks]
                diag = jnp.where(lane == s, jnp.sum(x, axis=1, keepdims=True), diag)
            causal = (lane >= trow + r0) if rev else (lane <= trow + r0)
            blocks.append(blk + jnp.where(causal, diag, 0.0))
        scores = jnp.concatenate(blocks, axis=0)
        o_intra = jnp.dot(scores.astype(bf16), vh, preferred_element_type=f32)
        o_ref[0, :, vs] = o_inter + o_intra

        k_end = (kh * jnp.exp(b_end - bh)).astype(bf16)
        state_ref[h] = st * jnp.exp(b_end) + lax.dot_general(
            vh, k_end, (((0,), (0,)), ((), ())), preferred_element_type=f32)


def gla_scans(p_l, p_s, wa, ba, *, ctx_len):
    B, R, _ = p_l.shape
    C = GLA_CHUNK
    n = R // C
    cc = ctx_len // C
    wide = GLA_HEADS * GLA_DK
    assert GLA_W == 2 * wide

    def scan(rev):
        chunk = (lambda c: jnp.where(c < cc, cc - 1 - c, n - 1 + cc - c)) if rev else (lambda c: c)
        d = int(rev)
        return pl.pallas_call(
            functools.partial(_gla_kernel, rev=rev),
            grid=(B, n),
            in_specs=[pl.BlockSpec((1, C, wide), lambda b, c: (b, chunk(c), 0)),
                      pl.BlockSpec((1, C, wide), lambda b, c: (b, chunk(c), 1)),
                      pl.BlockSpec((1, C, GLA_W), lambda b, c: (b, chunk(c), 1)),
                      pl.BlockSpec((1, C, 128), lambda b, c: (b, chunk(c), 0)),
                      pl.BlockSpec((GLA_GATE_RANK, wide), lambda b, c: (0, 0)),
                      pl.BlockSpec((1, wide), lambda b, c: (0, 0))],
            out_specs=pl.BlockSpec((1, C, GLA_W), lambda b, c: (b, chunk(c), 0)),
            out_shape=jax.ShapeDtypeStruct((B, R, GLA_W), f32),
            scratch_shapes=[pltpu.VMEM((GLA_HEADS, GLA_DV, GLA_DK), f32),
                            pltpu.VMEM((C, wide), f32)],
            compiler_params=_params("arbitrary", "arbitrary"),
            name="gla_scan_rev" if rev else "gla_scan_fwd",
        )(p_l, p_l, p_l, p_s, wa[d], ba[d])

    return scan(False), scan(True)


def _top16(s, ids):
    big = jnp.float32(1e9)
    vals, picks = [], []
    for _ in range(PEER_TOPK):
        m = jnp.max(s, axis=0, keepdims=True)
        pick = jnp.min(jnp.where(s == m, ids, big), axis=0, keepdims=True)
        vals.append(m)
        picks.append(pick)
        s = jnp.where(ids == pick, -jnp.inf, s)
    return jnp.concatenate(vals, axis=0), jnp.concatenate(picks, axis=0)


def _route_head(q, k1, k2):
    nt = (((1,), (1,)), ((), ()))
    s1 = lax.dot_general(k1, q[:, :PEER_HALF], nt, preferred_element_type=f32)
    s2 = lax.dot_general(k2, q[:, PEER_HALF:], nt, preferred_element_type=f32)
    key_id = lax.broadcasted_iota(jnp.int32, s1.shape, 0).astype(f32)
    v1, i1 = _top16(s1, key_id)
    v2, i2 = _top16(s2, key_id)
    half = PEER_TOPK // 2
    sub = lax.broadcasted_iota(jnp.int32, (PEER_TOPK, v1.shape[1]), 0).astype(f32)
    nb = lambda a: PEER_TOPK if a == 0 else half
    cand = jnp.concatenate([v1[a:a + 1] + v2[:nb(a)] for a in range(half)] + [v1[half:] + v2[0:1]], axis=0)
    cand_e = jnp.concatenate([i1[a:a + 1] * PEER_NKEYS + i2[:nb(a)] for a in range(half)]
                             + [i1[half:] * PEER_NKEYS + i2[0:1]], axis=0)
    pos = jnp.concatenate([a * PEER_TOPK + sub[:nb(a)] for a in range(half)]
                          + [(sub[:half] + half) * PEER_TOPK], axis=0)
    best, bpos = _top16(cand, pos)
    experts = [jnp.max(jnp.where(pos == bpos[r:r + 1], cand_e, -1.0), axis=0, keepdims=True)
               for r in range(PEER_TOPK)]
    e = jnp.exp(best - best[0:1])
    return jnp.concatenate(experts, axis=0).astype(jnp.int32), e / jnp.sum(e, axis=0, keepdims=True)


def _router_kernel(q_ref, k1_ref, k2_ref, idx_ref, g_ref):
    for h in range(ROUTE_HEADS_PER_STEP):
        q = q_ref[:, h * PEER_QDIM:(h + 1) * PEER_QDIM].astype(bf16)
        idx_ref[h], g_ref[h] = _route_head(q, k1_ref[h], k2_ref[h])


def peer_route(q, k1, k2):
    T = q.shape[0]
    tt = 128
    hs = ROUTE_HEADS_PER_STEP
    return pl.pallas_call(
        _router_kernel,
        grid=(T // tt, PEER_HEADS // hs),
        in_specs=[pl.BlockSpec((tt, hs * PEER_QDIM), lambda i, h: (i, h)),
                  pl.BlockSpec((hs, PEER_NKEYS, PEER_HALF), lambda i, h: (h, 0, 0)),
                  pl.BlockSpec((hs, PEER_NKEYS, PEER_HALF), lambda i, h: (h, 0, 0))],
        out_specs=[pl.BlockSpec((hs, PEER_TOPK, tt), lambda i, h: (h, 0, i)),
                   pl.BlockSpec((hs, PEER_TOPK, tt), lambda i, h: (h, 0, i))],
        out_shape=[jax.ShapeDtypeStruct((PEER_HEADS, PEER_TOPK, T), jnp.int32),
                   jax.ShapeDtypeStruct((PEER_HEADS, PEER_TOPK, T), f32)],
        compiler_params=_params("arbitrary", "arbitrary"),
        name="peer_route",
    )(q, k1, k2)


PEER_TT = 128
PEER_AHEAD = 3
PEER_SLOTS = PEER_AHEAD + 1
PEER_NEXT = 8
assert PEER_NEXT >= PEER_AHEAD and ((PEER_TT + PEER_NEXT) * PEER_SEL) % 1024 == 0
assert PEER_TT % PEER_SLOTS == 0


def _unpack_words(w):
    lo = pltpu.bitcast(w << 16, f32)
    hi = pltpu.bitcast(w & jnp.uint32(0xFFFF0000), f32)
    return lo, hi


def _sum_sublanes_8(parts):
    sub = lax.broadcasted_iota(jnp.int32, (8, 128), 0)
    dist = 4
    while len(parts) > 1:
        nxt = []
        half = len(parts) // 2
        for a in range(half):
            lo_rows = parts[a] + pltpu.roll(parts[a], 8 - dist, axis=0)
            hi_rows = parts[a + half] + pltpu.roll(parts[a + half], dist, axis=0)
            nxt.append(jnp.where((sub & dist) == 0, lo_rows, hi_rows))
        parts = nxt
        dist //= 2
    return parts[0]


def _peer_kernel(idx_ref, tab_ref, h_ref, g_ref, o_ref, *scratch):
    bufs, (sem, part_ref, w_ref) = scratch[:PEER_SLOTS], scratch[PEER_SLOTS:]
    NS = PEER_SEL
    step = pl.program_id(0)

    def issue(t, slot):
        for j in range(NS):
            e = idx_ref[t * NS + j]
            pltpu.make_async_copy(tab_ref.at[e], bufs[slot].at[:, :, j, :], sem.at[slot]).start(priority=j % 2)

    def wait(slot):
        pltpu.make_async_copy(bufs[slot], bufs[slot], sem.at[slot]).wait()

    @pl.when(step == 0)
    def _():
        for t in range(PEER_AHEAD):
            issue(t, t)

    lane = lax.broadcasted_iota(jnp.int32, (NS, PEER_TT), 1)

    def compute(t, slot):
        buf = bufs[slot]
        hrow = h_ref[t]
        hb = [jnp.broadcast_to(hrow[r:r + 1], (8, 128)) for r in range(16)]
        for jg in range(NS // 8):
            js = slice(jg * 8, (jg + 1) * 8)
            acc = None
            for s in range(8):
                lo, hi = _unpack_words(buf[0, s, js, :])
                term = lo * hb[2 * s] + hi * hb[2 * s + 1]
                acc = term if acc is None else acc + term
            part_ref[js, :] = acc
        act = jnp.sum(part_ref[...], axis=1, keepdims=True)
        gate = jnp.sum(jnp.where(lane == t, g_ref[...], 0.0), axis=1, keepdims=True)
        w = gate * (0.5 * act * (1.0 + lax.erf(act * 0.7071067811865476)))
        w_ref[...] = jnp.broadcast_to(w, (NS, 128))
        rows = []
        for s in range(8):
            acc_lo = acc_hi = None
            for jg in range(NS // 8):
                js = slice(jg * 8, (jg + 1) * 8)
                lo, hi = _unpack_words(buf[1, s, js, :])
                wv = w_ref[js, :]
                acc_lo = wv * lo if acc_lo is None else acc_lo + wv * lo
                acc_hi = wv * hi if acc_hi is None else acc_hi + wv * hi
            rows += [acc_lo, acc_hi]
        o_ref[t] = jnp.concatenate([_sum_sublanes_8(rows[0:8]), _sum_sublanes_8(rows[8:16])], axis=0)

    def group(tg, carry):
        for u in range(PEER_SLOTS):
            t = tg * PEER_SLOTS + u
            wait(u)
            issue(t + PEER_AHEAD, (u + PEER_AHEAD) % PEER_SLOTS)
            compute(t, u)
        return carry

    lax.fori_loop(0, PEER_TT // PEER_SLOTS, group, 0)

    @pl.when(step == pl.num_programs(0) - 1)
    def _():
        for t in range(PEER_AHEAD):
            wait(t)


def peer_experts(idx, table, h, gates):
    T = h.shape[0]
    tt = PEER_TT
    tiles = idx.reshape(T // tt, tt, PEER_SEL)
    idx_ext = jnp.concatenate([tiles, jnp.roll(tiles, -1, axis=0)[:, :PEER_NEXT]], axis=1).reshape(-1)
    return pl.pallas_call(
        _peer_kernel,
        grid=(T // tt,),
        in_specs=[pl.BlockSpec(((tt + PEER_NEXT) * PEER_SEL,), lambda i: (i,), memory_space=pltpu.SMEM),
                  pl.BlockSpec(memory_space=pl.ANY),
                  pl.BlockSpec((tt, 16, 128), lambda i: (i, 0, 0)),
                  pl.BlockSpec((PEER_SEL, tt), lambda i: (0, i))],
        out_specs=pl.BlockSpec((tt, 16, 128), lambda i: (i, 0, 0)),
        out_shape=jax.ShapeDtypeStruct((T, 16, 128), f32),
        scratch_shapes=[pltpu.VMEM((2, 8, PEER_SEL, 128), jnp.uint32)] * PEER_SLOTS + [
                        pltpu.SemaphoreType.DMA((PEER_SLOTS,)),
                        pltpu.VMEM((PEER_SEL, 128), f32),
                        pltpu.VMEM((PEER_SEL, 128), f32)],
        compiler_params=_params("arbitrary"),
        name="peer_experts",
    )(idx_ext, table, h, gates)


def _round_bf16_bits(x):
    bits = pltpu.bitcast(x, jnp.uint32)
    return bits + jnp.uint32(0x7FFF) + ((bits >> 16) & jnp.uint32(1))


def _pack_kernel(u_ref, v_ref, o_ref):
    te = u_ref.shape[0]
    for m, ref in enumerate((u_ref, v_ref)):
        for s in range(8):
            lo = _round_bf16_bits(ref[:, (2 * s) * 128:(2 * s + 1) * 128])
            hi = _round_bf16_bits(ref[:, (2 * s + 1) * 128:(2 * s + 2) * 128])
            o_ref[pl.ds(m * 8 + s, te, stride=16), :] = (lo >> 16) | (hi & jnp.uint32(0xFFFF0000))


def pack_expert_tables(u, v):
    E, D = u.shape
    te = 512
    out = pl.pallas_call(
        _pack_kernel,
        grid=(E // te,),
        in_specs=[pl.BlockSpec((te, D), lambda i: (i, 0)), pl.BlockSpec((te, D), lambda i: (i, 0))],
        out_specs=pl.BlockSpec((te * 16, 128), lambda i: (i, 0)),
        out_shape=jax.ShapeDtypeStruct((E * 16, 128), jnp.uint32),
        compiler_params=_params("arbitrary"),
        name="pack_expert_tables",
    )(u, v)
    return out.reshape(E, 2, 8, 128)


def peer_ffn(h2, h2_tiles, wq, k1, k2, table):
    T = h2.shape[0]
    q = mm(h2, wq, tn=1024)
    idx, gates = peer_route(q, k1.astype(bf16), k2.astype(bf16))
    idx_tok = idx.reshape(PEER_SEL, T).T
    return peer_experts(idx_tok, table, h2_tiles, gates.reshape(PEER_SEL, T))


def _rope_tables(n_ctx, seq, dim):
    quarter = dim // 4
    t = jnp.arange(seq, dtype=jnp.int32)
    inv_freq = ROPE_THETA ** (-jnp.arange(quarter, dtype=f32) / quarter)
    ang = jnp.concatenate([(t // GRID_W)[:, None].astype(f32) * inv_freq,
                           (t % GRID_W)[:, None].astype(f32) * inv_freq], axis=-1)
    cos = jnp.concatenate([jnp.ones((n_ctx, dim // 2), f32), jnp.cos(ang)], axis=0)
    sin = jnp.concatenate([jnp.zeros((n_ctx, dim // 2), f32), jnp.sin(ang)], axis=0)
    return cos, sin


def _rope_tile_tables(cos, sin):
    R, half = cos.shape
    z = jnp.zeros((R, 128 - 2 * half), f32)
    zh = jnp.zeros((R, half), f32)
    return (jnp.concatenate([cos, cos, z], axis=1),
            jnp.concatenate([-sin, zh, z], axis=1),
            jnp.concatenate([zh, sin, z], axis=1))


def _ada(vec, w, b):
    n = vec.shape[0]
    a = jnp.zeros((8, vec.shape[1]), f32).at[:n].set(jax.nn.silu(vec))
    return mm(a, w, tm=8, tn=1024)[:n] + b


def kernel(x, c, ctx, c_ctx, w_mod, b_mod, norm1_g, w_in, gqa_qn_g, gqa_kn_g, gla_wa2_f, gla_ba_f, gla_wa2_b, gla_ba_b, gla_on_g, mla_qn_g, mla_wuq, mla_kvn_g, mla_wukv, w_br_gqa, w_br_gla, w_br_mla, w_out, norm2_g, peer_wq, peer_k1, peer_k2, peer_u, peer_v, final_g):
    B, S, D = x.shape
    Sc = ctx.shape[1]
    depth = w_in.shape[0]
    R = Sc + S
    ctx_tiles = Sc // ROW_TILE
    tabs_a = _rope_tile_tables(*_rope_tables(Sc, S, HEAD_DIM))
    tabs_m = _rope_tile_tables(*_rope_tables(Sc, S, MLA_ROPE))
    xs = jnp.concatenate([ctx, x], axis=1)

    mods = []
    for i in range(depth):
        m = _ada(jnp.concatenate([c_ctx[None, :], c], axis=0), w_mod[i], b_mod[i])
        m = jnp.stack([jnp.broadcast_to(m[0], (B, 6 * D)), m[1:]], axis=1)
        mods.append([t[:, :, None, :] for t in jnp.split(m, 6, axis=-1)])

    h1, = norm_mod(xs, norm1_g[0], mods[0][1], mods[0][0], ctx_tiles=ctx_tiles, out_dtype=bf16)
    for i in range(depth):
        need_ctx = i < depth - 1
        sh1, sc1, g1, sh2, sc2, g2 = mods[i]

        h1 = h1.reshape(B * R, D)
        w = w_in[i]
        p_a = mm(h1, w[:, 0:1536], tn=768).reshape(B, R, 1536)
        p_l = mm(h1, w[:, 1536:4608], tn=1024).reshape(B, R, 3072)
        p_m = mm(h1, w[:, 4640:5664], tn=1024).reshape(B, R, 1024)
        w_small = jnp.concatenate([w[:, 5664:5728], w[:, 4608:4640], jnp.zeros((D, 32), f32)], axis=1)
        p_s = mm(h1, w_small, tn=128).reshape(B, R, 128)
        p_g = mm(h1, w[:, 5728:11872], tn=1024, out_dtype=bf16).reshape(B, R, 3 * D)

        aq, ak, av = gqa_prep(p_a, gqa_qn_g[i], gqa_kn_g[i], tabs_a)
        gqa = functools.partial(attention, heads=GQA_HEADS, kv_heads=GQA_KV_HEADS, dq=HEAD_DIM, dv=HEAD_DIM)
        if need_ctx:
            o_a = gqa(aq, ak, av, tq=ATT_ROWS, tk=R // 2, q_start=Sc, out_rows=R, out_start=Sc,
                      fill=jnp.zeros((B, R, GQA_W), f32))
            o_a = gqa(aq, ak[:, :Sc], av[:, :Sc], tq=Sc, tk=Sc, q_rows=Sc, out_rows=R, fill=o_a)
        else:
            o_a = gqa(aq, ak, av, tq=ATT_ROWS, tk=R // 2, q_start=Sc)

        wuq = jnp.pad(mla_wuq[i].reshape(MLA_Q_RANK, MLA_HEADS, MLA_NOPE + MLA_ROPE),
                      ((0, 0), (0, 0), (0, MLA_QK_PAD - MLA_NOPE - MLA_ROPE))).reshape(MLA_Q_RANK, MLA_HEADS * MLA_QK_PAD)
        p_m2 = p_m.reshape(B * R, MLA_Q_RANK + MLA_KV_RANK)
        mq = mm_rms(p_m2, 0, mla_qn_g[i], wuq, tn=1024).reshape(B, R, MLA_HEADS * MLA_QK_PAD)
        mkv = mm_rms(p_m2, 1, mla_kvn_g[i], mla_wukv[i], tn=1024).reshape(B, R, MLA_HEADS * (MLA_NOPE + MLA_V))
        m_q, m_k, m_v = mla_prep(mq, mkv, p_s, tabs_m)
        mla = functools.partial(attention, heads=MLA_HEADS, kv_heads=MLA_HEADS, dq=MLA_QK_PAD, dv=MLA_V)
        if need_ctx:
            o_m = mla(m_q[:, Sc:], m_k, m_v, tq=4 * ATT_ROWS, tk=R, out_rows=R + 3 * Sc, out_start=4 * Sc,
                      fill=jnp.zeros((B, R + 3 * Sc, MLA_W), f32))
            o_m = mla(m_q, m_k[:, :Sc], m_v[:, :Sc], tq=Sc, tk=Sc, q_rows=Sc, out_rows=R + 3 * Sc, out_start=3 * Sc,
                      fill=o_m)
        else:
            o_m = mla(m_q[:, Sc:], m_k, m_v, tq=4 * ATT_ROWS, tk=R)

        wa = jnp.stack([gla_wa2_f[i], gla_wa2_b[i]], axis=0).astype(bf16)
        ba = jnp.stack([gla_ba_f[i], gla_ba_b[i]], axis=0)[:, None, :]
        o_f, o_b = gla_scans(p_l, p_s, wa, ba, ctx_len=Sc)

        n_ctx = Sc if need_ctx else 0
        if not need_ctx:
            xs = xs[:, Sc:]
        y = merge_branches(o_a, o_f, o_b, p_l, o_m, p_g, w_br_gqa[i], w_br_gla[i], w_br_mla[i], gla_on_g[i],
                           row0=Sc - n_ctx, om_row0=3 * Sc if need_ctx else 0)
        xs = mm_residual(y, w_out[i], xs, g1, ctx_tiles=n_ctx // ROW_TILE)
        rows = xs.shape[1]

        h2, h2_tiles = norm_mod(xs, norm2_g[i], sc2, sh2, ctx_tiles=n_ctx // ROW_TILE, out_dtype=bf16, emit_tiled=True)
        table = pack_expert_tables(peer_u[i], peer_v[i])
        ff = peer_ffn(h2.reshape(B * rows, D), h2_tiles.reshape(B * rows, FEAT_ROWS, 128),
                      peer_wq[i], peer_k1[i], peer_k2[i], table).reshape(B * rows * FEAT_ROWS, 128)
        if need_ctx:
            xs, h1 = norm_mod(xs, norm1_g[i + 1], mods[i + 1][1], mods[i + 1][0], ctx_tiles=ctx_tiles, out_dtype=bf16,
                              res=ff, res_gate=g2, emit_x=True)
        else:
            zero = jnp.zeros((B, 2, 1, D), f32)
            out, = norm_mod(xs, final_g, zero, zero, ctx_tiles=0, out_dtype=f32, res=ff, res_gate=g2)
    return out
```

```python
import functools

import jax
import jax.numpy as jnp
from jax import lax
from jax.experimental import pallas as pl
from jax.experimental.pallas import tpu as pltpu

f32 = jnp.float32
bf16 = jnp.bfloat16

D_MODEL = 2048
GRID_W = 64
ROPE_THETA = 10000.0
NORM_EPS = 1e-6
LOG2E = 1.4426950408889634

GQA_HEADS = 8
GQA_KV_HEADS = 2
HEAD_DIM = 128
GLA_HEADS = 4
GLA_DK = 128
GLA_DV = 256
GLA_GATE_RANK = 16
GLA_TAU = 16.0
GLA_CHUNK = 64
GLA_SUB = 16
GLA_LR_COL = 64
MLA_HEADS = 8
MLA_Q_RANK = 512
MLA_KV_RANK = 512
MLA_NOPE = 128
MLA_ROPE = 64
MLA_V = 128
MLA_QK_PAD = 256
GQA_W = GQA_HEADS * HEAD_DIM
GLA_W = GLA_HEADS * GLA_DV
MLA_W = MLA_HEADS * MLA_V
PEER_HEADS = 8
PEER_NKEYS = 128
PEER_QDIM = 256
PEER_HALF = PEER_QDIM // 2
PEER_TOPK = 16
PEER_SEL = PEER_HEADS * PEER_TOPK
ROUTE_HEADS_PER_STEP = 4

VMEM_LIMIT = 48 * 1024 * 1024
ROW_TILE = 256


def _params(*sem):
    return pltpu.CompilerParams(dimension_semantics=sem, vmem_limit_bytes=VMEM_LIMIT)


def _mm_kernel(a_ref, b_ref, o_ref, bq_ref):
    @pl.when(pl.program_id(1) == 0)
    def _():
        bq_ref[...] = b_ref[...].astype(bf16)

    o_ref[...] = jnp.dot(a_ref[...].astype(bf16), bq_ref[...],
                         preferred_element_type=f32).astype(o_ref.dtype)


def mm(a, b, *, tm=512, tn=None, out_dtype=f32, layer=None):
    M, K = a.shape
    N = b.shape[-1]
    b_spec = (pl.BlockSpec((K, tn or N), lambda j, i: (0, j)) if layer is None
              else pl.BlockSpec((None, K, tn or N), lambda j, i: (layer, 0, j)))
    tn = N if tn is None else tn
    tm = min(tm, M)
    assert M % tm == 0 and N % tn == 0, (M, N, tm, tn)
    return pl.pallas_call(
        _mm_kernel,
        grid=(N // tn, M // tm),
        in_specs=[pl.BlockSpec((tm, K), lambda j, i: (i, 0)), b_spec],
        out_specs=pl.BlockSpec((tm, tn), lambda j, i: (i, j)),
        out_shape=jax.ShapeDtypeStruct((M, N), out_dtype),
        scratch_shapes=[pltpu.VMEM((K, tn), bf16)],
        compiler_params=_params("arbitrary", "arbitrary"),
        name="mm",
    )(a, b)


def _mm_rms_kernel(a_ref, g_ref, b_ref, o_ref, bq_ref):
    @pl.when(pl.program_id(1) == 0)
    def _():
        bq_ref[...] = b_ref[...].astype(bf16)

    a = a_ref[...]
    a = a * lax.rsqrt(jnp.mean(a * a, axis=-1, keepdims=True) + NORM_EPS) * g_ref[...]
    o_ref[...] = jnp.dot(a.astype(bf16), bq_ref[...], preferred_element_type=f32).astype(o_ref.dtype)


def mm_rms(a, a_col, g, b, *, tm=512, tn=None, out_dtype=f32):
    M = a.shape[0]
    K, N = b.shape
    tn = N if tn is None else tn
    assert M % tm == 0 and N % tn == 0
    return pl.pallas_call(
        _mm_rms_kernel,
        grid=(N // tn, M // tm),
        in_specs=[pl.BlockSpec((tm, K), lambda j, i: (i, a_col)),
                  pl.BlockSpec((1, K), lambda j, i: (0, 0)),
                  pl.BlockSpec((K, tn), lambda j, i: (0, j))],
        out_specs=pl.BlockSpec((tm, tn), lambda j, i: (i, j)),
        out_shape=jax.ShapeDtypeStruct((M, N), out_dtype),
        scratch_shapes=[pltpu.VMEM((K, tn), bf16)],
        compiler_params=_params("arbitrary", "arbitrary"),
        name="mm_rms",
    )(a, g.reshape(1, K), b)


FEAT_ROWS = D_MODEL // 128


def _norm_kernel(*refs, has_res, emit_x, emit_tiled):
    refs = list(refs)
    x_ref = refs.pop(0)
    x = x_ref[0]
    if has_res:
        ff_ref, gate_ref = refs.pop(0), refs.pop(0)
        ff = jnp.concatenate([ff_ref[pl.ds(r, ROW_TILE, stride=FEAT_ROWS), :] for r in range(FEAT_ROWS)], axis=1)
        x = x + gate_ref[0, 0] * ff
    g_ref, sc_ref, sh_ref = refs.pop(0), refs.pop(0), refs.pop(0)
    if emit_x:
        refs.pop(0)[0] = x
    y = x * lax.rsqrt(jnp.mean(x * x, axis=-1, keepdims=True) + NORM_EPS) * g_ref[...]
    h = y * (1.0 + sc_ref[0, 0]) + sh_ref[0, 0]
    h_ref = refs.pop(0)
    h_ref[0] = h.astype(h_ref.dtype)
    if emit_tiled:
        ht_ref = refs.pop(0)
        for r in range(FEAT_ROWS):
            ht_ref[pl.ds(r, ROW_TILE, stride=FEAT_ROWS), :] = h[:, r * 128:(r + 1) * 128]


def norm_mod(x, g, sc, sh, *, ctx_tiles, out_dtype, res=None, res_gate=None, emit_x=False, emit_tiled=False):
    B, R, D = x.shape
    nr = R // ROW_TILE
    kind = lambda r: jnp.where(r < ctx_tiles, 0, 1)
    row_spec = pl.BlockSpec((1, ROW_TILE, D), lambda b, r: (b, r, 0))
    mod_spec = pl.BlockSpec((1, 1, 1, D), lambda b, r: (b, kind(r), 0, 0))
    tile_spec = pl.BlockSpec((ROW_TILE * FEAT_ROWS, 128), lambda b, r: (b * nr + r, 0))
    args, in_specs = [x], [row_spec]
    if res is not None:
        args += [res, res_gate]
        in_specs += [tile_spec, mod_spec]
    args += [g.reshape(1, D), sc, sh]
    in_specs += [pl.BlockSpec((1, D), lambda b, r: (0, 0)), mod_spec, mod_spec]
    out_specs, out_shape = [], []
    if emit_x:
        out_specs.append(row_spec)
        out_shape.append(jax.ShapeDtypeStruct((B, R, D), f32))
    out_specs.append(row_spec)
    out_shape.append(jax.ShapeDtypeStruct((B, R, D), out_dtype))
    if emit_tiled:
        out_specs.append(tile_spec)
        out_shape.append(jax.ShapeDtypeStruct((B * R * FEAT_ROWS, 128), f32))
    return pl.pallas_call(
        functools.partial(_norm_kernel, has_res=res is not None, emit_x=emit_x, emit_tiled=emit_tiled),
        grid=(B, nr),
        in_specs=in_specs,
        out_specs=out_specs,
        out_shape=out_shape,
        compiler_params=_params("arbitrary", "arbitrary"),
        name="norm_mod",
    )(*args)


MERGE_TN = 1024


def _merge_kernel(oa_ref, of_ref, ob_ref, lg_ref, om_ref, ga_ref, gl_ref, gm_ref, wa_ref, wl_ref, wm_ref, gn_ref,
                  o_ref):
    o = of_ref[0] + ob_ref[0]
    lg = lg_ref[0]
    heads = []
    for h in range(GLA_HEADS):
        oh = o[:, h * GLA_DV:(h + 1) * GLA_DV]
        heads.append(oh * lax.rsqrt(jnp.mean(oh * oh, axis=-1, keepdims=True) + NORM_EPS) * gn_ref[...])
    o_l = jnp.concatenate(heads, axis=1) * (lg * jax.nn.sigmoid(lg))
    dot = lambda a, w: jnp.dot(a.astype(bf16), w[...], preferred_element_type=f32)
    sig = lambda r: jax.nn.sigmoid(r[0].astype(f32))
    y = (sig(ga_ref) * dot(oa_ref[0], wa_ref) + sig(gl_ref) * dot(o_l, wl_ref) + sig(gm_ref) * dot(om_ref[0], wm_ref))
    o_ref[0] = y.astype(o_ref.dtype)


def merge_branches(o_a, o_f, o_b, p_l, o_m, p_g, wa, wl, wm, gn, *, row0, om_row0=0):
    B, rows, _ = o_a.shape
    D = wa.shape[1]
    tm, tn = ROW_TILE, MERGE_TN
    nb = D // tn
    r0, rm = row0 // tm, om_row0 // tm
    own = lambda w, off=0: pl.BlockSpec((1, tm, w), lambda j, b, i: (b, i + off, 0))
    full = lambda w, cb: pl.BlockSpec((1, tm, w), lambda j, b, i: (b, i + r0, cb))
    gate = lambda k: pl.BlockSpec((1, tm, tn), lambda j, b, i: (b, i + r0, k * nb + j))
    wspec = pl.BlockSpec((GQA_W, tn), lambda j, b, i: (0, j))
    return pl.pallas_call(
        _merge_kernel,
        grid=(nb, B, rows // tm),
        in_specs=[own(GQA_W), full(GLA_W, 0), full(GLA_W, 0), full(GLA_W, 2), own(MLA_W, rm),
                  gate(0), gate(1), gate(2), wspec, wspec, wspec,
                  pl.BlockSpec((1, GLA_DV), lambda j, b, i: (0, 0))],
        out_specs=pl.BlockSpec((1, tm, tn), lambda j, b, i: (b, i, j)),
        out_shape=jax.ShapeDtypeStruct((B, rows, D), bf16),
        compiler_params=_params("arbitrary", "arbitrary", "arbitrary"),
        name="merge_branches",
    )(o_a, o_f, o_b, p_l, o_m, p_g, p_g, p_g, wa.astype(bf16), wl.astype(bf16), wm.astype(bf16), gn.reshape(1, GLA_DV))


def _mm_res_kernel(y_ref, w_ref, x_ref, g_ref, o_ref, wq_ref):
    @pl.when((pl.program_id(1) == 0) & (pl.program_id(2) == 0))
    def _():
        wq_ref[...] = w_ref[...].astype(bf16)

    o_ref[0] = x_ref[0] + g_ref[0, 0] * jnp.dot(y_ref[0], wq_ref[...], preferred_element_type=f32)


def mm_residual(y, w, x, gate, *, ctx_tiles, tn=1024):
    B, rows, K = y.shape
    N = w.shape[1]
    tm = ROW_TILE
    kind = lambda i: jnp.where(i < ctx_tiles, 0, 1)
    return pl.pallas_call(
        _mm_res_kernel,
        grid=(N // tn, B, rows // tm),
        in_specs=[pl.BlockSpec((1, tm, K), lambda j, b, i: (b, i, 0)),
                  pl.BlockSpec((K, tn), lambda j, b, i: (0, j)),
                  pl.BlockSpec((1, tm, tn), lambda j, b, i: (b, i, j)),
                  pl.BlockSpec((1, 1, 1, tn), lambda j, b, i: (b, kind(i), 0, j))],
        out_specs=pl.BlockSpec((1, tm, tn), lambda j, b, i: (b, i, j)),
        out_shape=jax.ShapeDtypeStruct((B, rows, N), f32),
        scratch_shapes=[pltpu.VMEM((K, tn), bf16)],
        compiler_params=_params("arbitrary", "arbitrary", "arbitrary"),
        name="mm_residual",
    )(y, w, x, gate)


def _rope_tile(y, tabs, half):
    c, s_lo, s_hi = tabs
    return y * c[...] + pltpu.roll(y, 128 - half, axis=1) * s_lo[...] + pltpu.roll(y, half, axis=1) * s_hi[...]


def _head_norm(x, g_ref):
    return x * lax.rsqrt(jnp.mean(x * x, axis=-1, keepdims=True) + NORM_EPS) * g_ref[...]


def _ones_column_tile(rows):
    lane = lax.broadcasted_iota(jnp.int32, (rows, 128), 1)
    return jnp.where(lane == 0, 1.0, 0.0).astype(bf16)


def _gqa_prep_kernel(p_ref, gq_ref, gk_ref, c_ref, s_lo_ref, s_hi_ref, q_ref, k_ref, v_ref):
    tabs = (c_ref, s_lo_ref, s_hi_ref)
    hd = HEAD_DIM
    q_scale = HEAD_DIM ** -0.5 * LOG2E
    for h in range(GQA_HEADS):
        x = p_ref[0, :, h * hd:(h + 1) * hd]
        q_ref[0, :, h * hd:(h + 1) * hd] = (_rope_tile(_head_norm(x, gq_ref), tabs, hd // 2) * q_scale).astype(bf16)
    k0 = GQA_HEADS * hd
    v0 = k0 + GQA_KV_HEADS * hd
    ones = _ones_column_tile(p_ref.shape[1])
    for h in range(GQA_KV_HEADS):
        x = p_ref[0, :, k0 + h * hd:k0 + (h + 1) * hd]
        k_ref[0, :, h * hd:(h + 1) * hd] = _rope_tile(_head_norm(x, gk_ref), tabs, hd // 2).astype(bf16)
        v_ref[0, :, 2 * h * hd:(2 * h + 1) * hd] = p_ref[0, :, v0 + h * hd:v0 + (h + 1) * hd].astype(bf16)
        v_ref[0, :, (2 * h + 1) * hd:(2 * h + 2) * hd] = ones


def gqa_prep(p_a, gq, gk, tabs):
    B, R, W = p_a.shape
    tm = ROW_TILE
    tab_spec = pl.BlockSpec((tm, 128), lambda b, r: (r, 0))
    gain_spec = pl.BlockSpec((1, HEAD_DIM), lambda b, r: (0, 0))
    row = lambda w: pl.BlockSpec((1, tm, w), lambda b, r: (b, r, 0))
    widths = (GQA_W, GQA_KV_HEADS * HEAD_DIM, GQA_KV_HEADS * 2 * HEAD_DIM)
    return pl.pallas_call(
        _gqa_prep_kernel,
        grid=(B, R // tm),
        in_specs=[row(W), gain_spec, gain_spec, tab_spec, tab_spec, tab_spec],
        out_specs=[row(w) for w in widths],
        out_shape=[jax.ShapeDtypeStruct((B, R, w), bf16) for w in widths],
        compiler_params=_params("arbitrary", "arbitrary"),
        name="gqa_prep",
    )(p_a, gq.reshape(1, HEAD_DIM), gk.reshape(1, HEAD_DIM), *tabs)


def _mla_prep_kernel(mq_ref, mkv_ref, ps_ref, c_ref, s_lo_ref, s_hi_ref, q_ref, k_ref, v_ref):
    tabs = (c_ref, s_lo_ref, s_hi_ref)
    half = MLA_ROPE // 2
    q_scale = (MLA_NOPE + MLA_ROPE) ** -0.5 * LOG2E
    k_rope = _rope_tile(ps_ref[0], tabs, half).astype(bf16)
    ones = _ones_column_tile(mq_ref.shape[1])
    P = MLA_QK_PAD
    for h in range(MLA_HEADS):
        lo, mid, hi = h * P, h * P + 128, (h + 1) * P
        q_ref[0, :, lo:mid] = (mq_ref[0, :, lo:mid] * q_scale).astype(bf16)
        q_ref[0, :, mid:hi] = (_rope_tile(mq_ref[0, :, mid:hi], tabs, half) * q_scale).astype(bf16)
        k_ref[0, :, lo:mid] = mkv_ref[0, :, lo:mid].astype(bf16)
        k_ref[0, :, mid:hi] = k_rope
        v_ref[0, :, lo:mid] = mkv_ref[0, :, mid:hi].astype(bf16)
        v_ref[0, :, mid:hi] = ones


def mla_prep(mq, mkv, p_s, tabs):
    B, R, W = mq.shape
    tm = ROW_TILE
    tab_spec = pl.BlockSpec((tm, 128), lambda b, r: (r, 0))
    row = lambda w: pl.BlockSpec((1, tm, w), lambda b, r: (b, r, 0))
    return pl.pallas_call(
        _mla_prep_kernel,
        grid=(B, R // tm),
        in_specs=[row(W), row(W), row(128), tab_spec, tab_spec, tab_spec],
        out_specs=[row(W)] * 3,
        out_shape=[jax.ShapeDtypeStruct((B, R, W), bf16)] * 3,
        compiler_params=_params("arbitrary", "arbitrary"),
        name="mla_prep",
    )(mq, mkv, p_s, *tabs)


ATT_ROWS = 256


def _attn_kernel(q_ref, k_ref, v_ref, o_ref, m_ref, acc_ref, *, tk, nk, group, splits, dq, dv):
    R = ATT_ROWS
    chains = [(g, r) for g in range(group) for r in range(splits)]
    m_ref[...] = jnp.full(m_ref.shape, -jnp.inf, f32)
    acc_ref[...] = jnp.zeros(acc_ref.shape, f32)

    def body(c, carry):
        off = pl.multiple_of(c * tk, tk)
        kc = k_ref[0, pl.ds(off, tk), :]
        vc = v_ref[0, pl.ds(off, tk), :]
        for n, (g, r) in enumerate(chains):
            q = q_ref[0, r * R:(r + 1) * R, g * dq:(g + 1) * dq]
            s = lax.dot_general(q, kc, (((1,), (1,)), ((), ())), preferred_element_type=f32)
            m_old = m_ref[n]
            m_new = jnp.maximum(m_old, jnp.max(s, axis=1, keepdims=True))
            p = jnp.exp2(s - m_new)
            alpha = jnp.exp2(m_old - m_new)
            acc_ref[n] = alpha * acc_ref[n] + jnp.dot(p.astype(bf16), vc, preferred_element_type=f32)
            m_ref[n] = m_new
        return carry

    lax.fori_loop(0, nk, body, 0)
    for n, (g, r) in enumerate(chains):
        acc = acc_ref[n]
        o_ref[0, r * R:(r + 1) * R, g * dv:(g + 1) * dv] = (acc[:, :dv] / acc[:, dv:dv + 1]).astype(o_ref.dtype)


def _attn_fill_kernel(kern, q_ref, k_ref, v_ref, fill_ref, o_ref, *scratch):
    del fill_ref
    kern(q_ref, k_ref, v_ref, o_ref, *scratch)


def attention(q, k, v, *, heads, kv_heads, dq, dv, tq, tk, q_start=0, q_rows=None, out_rows=None, out_start=0,
              fill=None):
    B, Sq, _ = q.shape
    Sk = k.shape[1]
    q_rows = Sq - q_start if q_rows is None else q_rows
    out_rows = q_rows if out_rows is None else out_rows
    group = heads // kv_heads
    splits = tq // ATT_ROWS
    assert q_rows % tq == 0 and q_start % tq == 0 and out_start % tq == 0 and Sk % tk == 0 and tq % ATT_ROWS == 0
    first, out_first = q_start // tq, out_start // tq
    chains = group * splits
    in_specs = [pl.BlockSpec((1, tq, group * dq), lambda b, h, i: (b, i + first, h)),
                pl.BlockSpec((1, Sk, dq), lambda b, h, i: (b, 0, h)),
                pl.BlockSpec((1, Sk, 2 * dv), lambda b, h, i: (b, 0, h))]
    args = [q, k, v]
    kern = functools.partial(_attn_kernel, tk=tk, nk=Sk // tk, group=group, splits=splits, dq=dq, dv=dv)
    if fill is not None:
        in_specs.append(pl.BlockSpec(memory_space=pl.ANY))
        args.append(fill)
        kern = functools.partial(_attn_fill_kernel, kern)
    return pl.pallas_call(
        kern,
        grid=(B, kv_heads, q_rows // tq),
        in_specs=in_specs,
        out_specs=pl.BlockSpec((1, tq, group * dv), lambda b, h, i: (b, i + out_first, h)),
        out_shape=jax.ShapeDtypeStruct((B, out_rows, heads * dv), f32),
        scratch_shapes=[pltpu.VMEM((chains, ATT_ROWS, 1), f32),
                        pltpu.VMEM((chains, ATT_ROWS, 2 * dv), f32)],
        input_output_aliases={3: 0} if fill is not None else {},
        compiler_params=_params("arbitrary", "arbitrary", "arbitrary"),
        name="attention",
    )(*args)


def _exact_dot_01(t01, x):
    h1 = x.astype(bf16)
    r1 = x - h1.astype(f32)
    h2 = r1.astype(bf16)
    h3 = (r1 - h2.astype(f32)).astype(bf16)
    d = lambda h: jnp.dot(t01, h, preferred_element_type=f32)
    return d(h1) + d(h2) + d(h3)


def _gla_kernel(q_ref, k_ref, v_ref, lr_ref, wa_ref, ba_ref, o_ref, state_ref, b_ref, *, rev):
    c = pl.program_id(1)
    C, SB = GLA_CHUNK, GLA_SUB
    nt = (((1,), (1,)), ((), ()))

    @pl.when(c == 0)
    def _():
        state_ref[...] = jnp.zeros_like(state_ref)

    row = lax.broadcasted_iota(jnp.int32, (C, C), 0)
    col = lax.broadcasted_iota(jnp.int32, (C, C), 1)
    tri = (row <= col) if rev else (row >= col)
    lr0 = GLA_LR_COL + int(rev) * GLA_GATE_RANK
    lr = lr_ref[0][:, lr0:lr0 + GLA_GATE_RANK]
    pre = jnp.dot(lr.astype(bf16), wa_ref[...], preferred_element_type=f32) + ba_ref[...]
    log_a = (jnp.minimum(pre, 0.0) - jnp.log1p(jnp.exp(-jnp.abs(pre)))) * (1.0 / GLA_TAU)
    b_ref[...] = _exact_dot_01(tri.astype(bf16), log_a)
    end = 0 if rev else C - 1
    krow = lax.broadcasted_iota(jnp.int32, (C, GLA_DK), 0)
    lane = lax.broadcasted_iota(jnp.int32, (SB, C), 1)
    trow = lax.broadcasted_iota(jnp.int32, (SB, C), 0)

    for h in range(GLA_HEADS):
        ks = slice(h * GLA_DK, (h + 1) * GLA_DK)
        vs = slice(h * GLA_DV, (h + 1) * GLA_DV)
        qh = q_ref[0, :, ks] * (GLA_DK ** -0.5)
        kh = k_ref[0, :, ks]
        vh = v_ref[0, :, vs].astype(bf16)
        bh = b_ref[:, ks]
        b_end = b_ref[end:end + 1, ks]
        st = state_ref[h]

        o_inter = lax.dot_general((qh * jnp.exp(bh)).astype(bf16), st.astype(bf16), nt,
                                  preferred_element_type=f32)

        blocks = []
        for i in range(C // SB):
            r0 = i * SB
            qi, bi = qh[r0:r0 + SB], bh[r0:r0 + SB]
            has_earlier = i < C // SB - 1 if rev else i > 0
            if has_earlier:
                first = r0 + SB - 1 if rev else r0
                b_first = b_ref[first:first + 1, ks]
                earlier = (krow >= r0 + SB) if rev else (krow < r0)
                k_dec = jnp.where(earlier, kh * jnp.exp(jnp.minimum(b_first - bh, 0.0)), 0.0)
                q_dec = qi * jnp.exp(jnp.minimum(bi - b_first, 0.0))
                blk = lax.dot_general(q_dec.astype(bf16), k_dec.astype(bf16), nt, preferred_element_type=f32)
            else:
                blk = jnp.zeros((SB, C), f32)
            diag = jnp.zeros((SB, C), f32)
            for u in range(SB):
                s = r0 + u
                x = qi * jnp.exp(jnp.minimum(bi - b_ref[s:s + 1, ks], 0.0)) * k_ref[0, s:s + 1, ks]
                diag = jnp.where(lane == s, jnp.sum(x, axis=1, keepdims=True), diag)
            causal = (lane >= trow + r0) if rev else (lane <= trow + r0)
            blocks.append(blk + jnp.where(causal, diag, 0.0))
        scores = jnp.concatenate(blocks, axis=0)
        o_intra = jnp.dot(scores.astype(bf16), vh, preferred_element_type=f32)
        o_ref[0, :, vs] = o_inter + o_intra

        k_end = (kh * jnp.exp(b_end - bh)).astype(bf16)
        state_ref[h] = st * jnp.exp(b_end) + lax.dot_general(
            vh, k_end, (((0,), (0,)), ((), ())), preferred_element_type=f32)


def gla_scans(p_l, p_s, wa, ba, *, ctx_len):
    B, R, _ = p_l.shape
    C = GLA_CHUNK
    n = R // C
    cc = ctx_len // C
    wide = GLA_HEADS * GLA_DK
    assert GLA_W == 2 * wide

    def scan(rev):
        chunk = (lambda c: jnp.where(c < cc, cc - 1 - c, n - 1 + cc - c)) if rev else (lambda c: c)
        d = int(rev)
        return pl.pallas_call(
            functools.partial(_gla_kernel, rev=rev),
            grid=(B, n),
            in_specs=[pl.BlockSpec((1, C, wide), lambda b, c: (b, chunk(c), 0)),
                      pl.BlockSpec((1, C, wide), lambda b, c: (b, chunk(c), 1)),
                      pl.BlockSpec((1, C, GLA_W), lambda b, c: (b, chunk(c), 1)),
                      pl.BlockSpec((1, C, 128), lambda b, c: (b, chunk(c), 0)),
                      pl.BlockSpec((GLA_GATE_RANK, wide), lambda b, c: (0, 0)),
                      pl.BlockSpec((1, wide), lambda b, c: (0, 0))],
            out_specs=pl.BlockSpec((1, C, GLA_W), lambda b, c: (b, chunk(c), 0)),
            out_shape=jax.ShapeDtypeStruct((B, R, GLA_W), f32),
            scratch_shapes=[pltpu.VMEM((GLA_HEADS, GLA_DV, GLA_DK), f32),
                            pltpu.VMEM((C, wide), f32)],
            compiler_params=_params("arbitrary", "arbitrary"),
            name="gla_scan_rev" if rev else "gla_scan_fwd",
        )(p_l, p_l, p_l, p_s, wa[d], ba[d])

    return scan(False), scan(True)


def _top16(s, ids):
    big = jnp.float32(1e9)
    vals, picks = [], []
    for _ in range(PEER_TOPK):
        m = jnp.max(s, axis=0, keepdims=True)
        pick = jnp.min(jnp.where(s == m, ids, big), axis=0, keepdims=True)
        vals.append(m)
        picks.append(pick)
        s = jnp.where(ids == pick, -jnp.inf, s)
    return jnp.concatenate(vals, axis=0), jnp.concatenate(picks, axis=0)


def _route_head(q, k1, k2):
    nt = (((1,), (1,)), ((), ()))
    s1 = lax.dot_general(k1, q[:, :PEER_HALF], nt, preferred_element_type=f32)
    s2 = lax.dot_general(k2, q[:, PEER_HALF:], nt, preferred_element_type=f32)
    key_id = lax.broadcasted_iota(jnp.int32, s1.shape, 0).astype(f32)
    v1, i1 = _top16(s1, key_id)
    v2, i2 = _top16(s2, key_id)
    half = PEER_TOPK // 2
    sub = lax.broadcasted_iota(jnp.int32, (PEER_TOPK, v1.shape[1]), 0).astype(f32)
    nb = lambda a: PEER_TOPK if a == 0 else half
    cand = jnp.concatenate([v1[a:a + 1] + v2[:nb(a)] for a in range(half)] + [v1[half:] + v2[0:1]], axis=0)
    cand_e = jnp.concatenate([i1[a:a + 1] * PEER_NKEYS + i2[:nb(a)] for a in range(half)]
                             + [i1[half:] * PEER_NKEYS + i2[0:1]], axis=0)
    pos = jnp.concatenate([a * PEER_TOPK + sub[:nb(a)] for a in range(half)]
                          + [(sub[:half] + half) * PEER_TOPK], axis=0)
    best, bpos = _top16(cand, pos)
    experts = [jnp.max(jnp.where(pos == bpos[r:r + 1], cand_e, -1.0), axis=0, keepdims=True)
               for r in range(PEER_TOPK)]
    e = jnp.exp(best - best[0:1])
    return jnp.concatenate(experts, axis=0).astype(jnp.int32), e / jnp.sum(e, axis=0, keepdims=True)


def _router_kernel(q_ref, k1_ref, k2_ref, idx_ref, g_ref):
    for h in range(ROUTE_HEADS_PER_STEP):
        q = q_ref[:, h * PEER_QDIM:(h + 1) * PEER_QDIM].astype(bf16)
        idx_ref[h], g_ref[h] = _route_head(q, k1_ref[h], k2_ref[h])


def peer_route(q, k1, k2):
    T = q.shape[0]
    tt = 128
    hs = ROUTE_HEADS_PER_STEP
    return pl.pallas_call(
        _router_kernel,
        grid=(T // tt, PEER_HEADS // hs),
        in_specs=[pl.BlockSpec((tt, hs * PEER_QDIM), lambda i, h: (i, h)),
                  pl.BlockSpec((hs, PEER_NKEYS, PEER_HALF), lambda i, h: (h, 0, 0)),
                  pl.BlockSpec((hs, PEER_NKEYS, PEER_HALF), lambda i, h: (h, 0, 0))],
        out_specs=[pl.BlockSpec((hs, PEER_TOPK, tt), lambda i, h: (h, 0, i)),
                   pl.BlockSpec((hs, PEER_TOPK, tt), lambda i, h: (h, 0, i))],
        out_shape=[jax.ShapeDtypeStruct((PEER_HEADS, PEER_TOPK, T), jnp.int32),
                   jax.ShapeDtypeStruct((PEER_HEADS, PEER_TOPK, T), f32)],
        compiler_params=_params("arbitrary", "arbitrary"),
        name="peer_route",
    )(q, k1, k2)


PEER_TT = 128
PEER_AHEAD = 3
PEER_SLOTS = PEER_AHEAD + 1
PEER_NEXT = 8
assert PEER_NEXT >= PEER_AHEAD and ((PEER_TT + PEER_NEXT) * PEER_SEL) % 1024 == 0
assert PEER_TT % PEER_SLOTS == 0


def _unpack_words(w):
    lo = pltpu.bitcast(w << 16, f32)
    hi = pltpu.bitcast(w & jnp.uint32(0xFFFF0000), f32)
    return lo, hi


def _sum_sublanes_8(parts):
    sub = lax.broadcasted_iota(jnp.int32, (8, 128), 0)
    dist = 4
    while len(parts) > 1:
        nxt = []
        half = len(parts) // 2
        for a in range(half):
            lo_rows = parts[a] + pltpu.roll(parts[a], 8 - dist, axis=0)
            hi_rows = parts[a + half] + pltpu.roll(parts[a + half], dist, axis=0)
            nxt.append(jnp.where((sub & dist) == 0, lo_rows, hi_rows))
        parts = nxt
        dist //= 2
    return parts[0]


def _peer_kernel(idx_ref, tab_ref, h_ref, g_ref, o_ref, *scratch):
    bufs, (sem, part_ref, w_ref) = scratch[:PEER_SLOTS], scratch[PEER_SLOTS:]
    NS = PEER_SEL
    step = pl.program_id(0)

    def issue(t, slot):
        for j in range(NS):
            e = idx_ref[t * NS + j]
            pltpu.make_async_copy(tab_ref.at[e], bufs[slot].at[:, :, j, :], sem.at[slot]).start(priority=j % 2)

    def wait(slot):
        pltpu.make_async_copy(bufs[slot], bufs[slot], sem.at[slot]).wait()

    @pl.when(step == 0)
    def _():
        for t in range(PEER_AHEAD):
            issue(t, t)

    lane = lax.broadcasted_iota(jnp.int32, (NS, PEER_TT), 1)

    def compute(t, slot):
        buf = bufs[slot]
        hrow = h_ref[t]
        hb = [jnp.broadcast_to(hrow[r:r + 1], (8, 128)) for r in range(16)]
        for jg in range(NS // 8):
            js = slice(jg * 8, (jg + 1) * 8)
            acc = None
            for s in range(8):
                lo, hi = _unpack_words(buf[0, s, js, :])
                term = lo * hb[2 * s] + hi * hb[2 * s + 1]
                acc = term if acc is None else acc + term
            part_ref[js, :] = acc
        act = jnp.sum(part_ref[...], axis=1, keepdims=True)
        gate = jnp.sum(jnp.where(lane == t, g_ref[...], 0.0), axis=1, keepdims=True)
        w = gate * (0.5 * act * (1.0 + lax.erf(act * 0.7071067811865476)))
        w_ref[...] = jnp.broadcast_to(w, (NS, 128))
        rows = []
        for s in range(8):
            acc_lo = acc_hi = None
            for jg in range(NS // 8):
                js = slice(jg * 8, (jg + 1) * 8)
                lo, hi = _unpack_words(buf[1, s, js, :])
                wv = w_ref[js, :]
                acc_lo = wv * lo if acc_lo is None else acc_lo + wv * lo
                acc_hi = wv * hi if acc_hi is None else acc_hi + wv * hi
            rows += [acc_lo, acc_hi]
        o_ref[t] = jnp.concatenate([_sum_sublanes_8(rows[0:8]), _sum_sublanes_8(rows[8:16])], axis=0)

    def group(tg, carry):
        for u in range(PEER_SLOTS):
            t = tg * PEER_SLOTS + u
            wait(u)
            issue(t + PEER_AHEAD, (u + PEER_AHEAD) % PEER_SLOTS)
            compute(t, u)
        return carry

    lax.fori_loop(0, PEER_TT // PEER_SLOTS, group, 0)

    @pl.when(step == pl.num_programs(0) - 1)
    def _():
        for t in range(PEER_AHEAD):
            wait(t)


def peer_experts(idx, table, h, gates):
    T = h.shape[0]
    tt = PEER_TT
    tiles = idx.reshape(T // tt, tt, PEER_SEL)
    idx_ext = jnp.concatenate([tiles, jnp.roll(tiles, -1, axis=0)[:, :PEER_NEXT]], axis=1).reshape(-1)
    return pl.pallas_call(
        _peer_kernel,
        grid=(T // tt,),
        in_specs=[pl.BlockSpec(((tt + PEER_NEXT) * PEER_SEL,), lambda i: (i,), memory_space=pltpu.SMEM),
                  pl.BlockSpec(memory_space=pl.ANY),
                  pl.BlockSpec((tt, 16, 128), lambda i: (i, 0, 0)),
                  pl.BlockSpec((PEER_SEL, tt), lambda i: (0, i))],
        out_specs=pl.BlockSpec((tt, 16, 128), lambda i: (i, 0, 0)),
        out_shape=jax.ShapeDtypeStruct((T, 16, 128), f32),
        scratch_shapes=[pltpu.VMEM((2, 8, PEER_SEL, 128), jnp.uint32)] * PEER_SLOTS + [
                        pltpu.SemaphoreType.DMA((PEER_SLOTS,)),
                        pltpu.VMEM((PEER_SEL, 128), f32),
                        pltpu.VMEM((PEER_SEL, 128), f32)],
        compiler_params=_params("arbitrary"),
        name="peer_experts",
    )(idx_ext, table, h, gates)


def _round_bf16_bits(x):
    bits = pltpu.bitcast(x, jnp.uint32)
    return bits + jnp.uint32(0x7FFF) + ((bits >> 16) & jnp.uint32(1))


def _pack_kernel(u_ref, v_ref, o_ref):
    te = u_ref.shape[0]
    for m, ref in enumerate((u_ref, v_ref)):
        for s in range(8):
            lo = _round_bf16_bits(ref[:, (2 * s) * 128:(2 * s + 1) * 128])
            hi = _round_bf16_bits(ref[:, (2 * s + 1) * 128:(2 * s + 2) * 128])
            o_ref[pl.ds(m * 8 + s, te, stride=16), :] = (lo >> 16) | (hi & jnp.uint32(0xFFFF0000))


def pack_expert_tables(u, v, layer):
    _, E, D = u.shape
    te = 512
    out = pl.pallas_call(
        _pack_kernel,
        grid=(E // te,),
        in_specs=[pl.BlockSpec((None, te, D), lambda i: (layer, i, 0))] * 2,
        out_specs=pl.BlockSpec((te * 16, 128), lambda i: (i, 0)),
        out_shape=jax.ShapeDtypeStruct((E * 16, 128), jnp.uint32),
        compiler_params=_params("arbitrary"),
        name="pack_expert_tables",
    )(u, v)
    return out.reshape(E, 2, 8, 128)


def peer_ffn(h2, h2_tiles, wq, layer, k1, k2, table):
    T = h2.shape[0]
    q = mm(h2, wq, tn=1024, layer=layer)
    idx, gates = peer_route(q, k1.astype(bf16), k2.astype(bf16))
    idx_tok = idx.reshape(PEER_SEL, T).T
    return peer_experts(idx_tok, table, h2_tiles, gates.reshape(PEER_SEL, T))


def _rope_tables(n_ctx, seq, dim):
    quarter = dim // 4
    t = jnp.arange(seq, dtype=jnp.int32)
    inv_freq = ROPE_THETA ** (-jnp.arange(quarter, dtype=f32) / quarter)
    ang = jnp.concatenate([(t // GRID_W)[:, None].astype(f32) * inv_freq,
                           (t % GRID_W)[:, None].astype(f32) * inv_freq], axis=-1)
    cos = jnp.concatenate([jnp.ones((n_ctx, dim // 2), f32), jnp.cos(ang)], axis=0)
    sin = jnp.concatenate([jnp.zeros((n_ctx, dim // 2), f32), jnp.sin(ang)], axis=0)
    return cos, sin


def _rope_tile_tables(cos, sin):
    R, half = cos.shape
    z = jnp.zeros((R, 128 - 2 * half), f32)
    zh = jnp.zeros((R, half), f32)
    return (jnp.concatenate([cos, cos, z], axis=1),
            jnp.concatenate([-sin, zh, z], axis=1),
            jnp.concatenate([zh, sin, z], axis=1))


def _ada(vec, w, layer, b):
    n = vec.shape[0]
    a = jnp.zeros((8, vec.shape[1]), f32).at[:n].set(jax.nn.silu(vec))
    return mm(a, w, tm=8, tn=1024, layer=layer)[:n] + b


def kernel(x, c, ctx, c_ctx, w_mod, b_mod, norm1_g, w_in, gqa_qn_g, gqa_kn_g, gla_wa2_f, gla_ba_f, gla_wa2_b, gla_ba_b, gla_on_g, mla_qn_g, mla_wuq, mla_kvn_g, mla_wukv, w_br_gqa, w_br_gla, w_br_mla, w_out, norm2_g, peer_wq, peer_k1, peer_k2, peer_u, peer_v, final_g):
    B, S, D = x.shape
    Sc = ctx.shape[1]
    depth = w_in.shape[0]
    R = Sc + S
    ctx_tiles = Sc // ROW_TILE
    tabs_a = _rope_tile_tables(*_rope_tables(Sc, S, HEAD_DIM))
    tabs_m = _rope_tile_tables(*_rope_tables(Sc, S, MLA_ROPE))
    xs = jnp.concatenate([ctx, x], axis=1)

    mods = []
    for i in range(depth):
        m = _ada(jnp.concatenate([c_ctx[None, :], c], axis=0), w_mod, i, b_mod[i])
        m = jnp.stack([jnp.broadcast_to(m[0], (B, 6 * D)), m[1:]], axis=1)
        mods.append([t[:, :, None, :] for t in jnp.split(m, 6, axis=-1)])

    h1, = norm_mod(xs, norm1_g[0], mods[0][1], mods[0][0], ctx_tiles=ctx_tiles, out_dtype=bf16)
    for i in range(depth):
        need_ctx = i < depth - 1
        sh1, sc1, g1, sh2, sc2, g2 = mods[i]

        h1 = h1.reshape(B * R, D)
        w = w_in[i]
        p_a = mm(h1, w[:, 0:1536], tn=768).reshape(B, R, 1536)
        p_l = mm(h1, w[:, 1536:4608], tn=1024).reshape(B, R, 3072)
        p_m = mm(h1, w[:, 4640:5664], tn=1024).reshape(B, R, 1024)
        w_small = jnp.concatenate([w[:, 5664:5728], w[:, 4608:4640], jnp.zeros((D, 32), f32)], axis=1)
        p_s = mm(h1, w_small, tn=128).reshape(B, R, 128)
        p_g = mm(h1, w[:, 5728:11872], tn=1024, out_dtype=bf16).reshape(B, R, 3 * D)

        aq, ak, av = gqa_prep(p_a, gqa_qn_g[i], gqa_kn_g[i], tabs_a)
        gqa = functools.partial(attention, heads=GQA_HEADS, kv_heads=GQA_KV_HEADS, dq=HEAD_DIM, dv=HEAD_DIM)
        if need_ctx:
            o_a = gqa(aq, ak, av, tq=ATT_ROWS, tk=R // 2, q_start=Sc, out_rows=R, out_start=Sc,
                      fill=jnp.zeros((B, R, GQA_W), f32))
            o_a = gqa(aq, ak[:, :Sc], av[:, :Sc], tq=Sc, tk=Sc, q_rows=Sc, out_rows=R, fill=o_a)
        else:
            o_a = gqa(aq, ak, av, tq=ATT_ROWS, tk=R // 2, q_start=Sc)

        wuq = jnp.pad(mla_wuq[i].reshape(MLA_Q_RANK, MLA_HEADS, MLA_NOPE + MLA_ROPE),
                      ((0, 0), (0, 0), (0, MLA_QK_PAD - MLA_NOPE - MLA_ROPE))).reshape(MLA_Q_RANK, MLA_HEADS * MLA_QK_PAD)
        p_m2 = p_m.reshape(B * R, MLA_Q_RANK + MLA_KV_RANK)
        mq = mm_rms(p_m2, 0, mla_qn_g[i], wuq, tn=1024).reshape(B, R, MLA_HEADS * MLA_QK_PAD)
        mkv = mm_rms(p_m2, 1, mla_kvn_g[i], mla_wukv[i], tn=1024).reshape(B, R, MLA_HEADS * (MLA_NOPE + MLA_V))
        m_q, m_k, m_v = mla_prep(mq, mkv, p_s, tabs_m)
        mla = functools.partial(attention, heads=MLA_HEADS, kv_heads=MLA_HEADS, dq=MLA_QK_PAD, dv=MLA_V)
        if need_ctx:
            o_m = mla(m_q[:, Sc:], m_k, m_v, tq=4 * ATT_ROWS, tk=R, out_rows=R + 3 * Sc, out_start=4 * Sc,
                      fill=jnp.zeros((B, R + 3 * Sc, MLA_W), f32))
            o_m = mla(m_q, m_k[:, :Sc], m_v[:, :Sc], tq=Sc, tk=Sc, q_rows=Sc, out_rows=R + 3 * Sc, out_start=3 * Sc,
                      fill=o_m)
        else:
            o_m = mla(m_q[:, Sc:], m_k, m_v, tq=4 * ATT_ROWS, tk=R)

        wa = jnp.stack([gla_wa2_f[i], gla_wa2_b[i]], axis=0).astype(bf16)
        ba = jnp.stack([gla_ba_f[i], gla_ba_b[i]], axis=0)[:, None, :]
        o_f, o_b = gla_scans(p_l, p_s, wa, ba, ctx_len=Sc)

        n_ctx = Sc if need_ctx else 0
        if not need_ctx:
            xs = xs[:, Sc:]
        y = merge_branches(o_a, o_f, o_b, p_l, o_m, p_g, w_br_gqa[i], w_br_gla[i], w_br_mla[i], gla_on_g[i],
                           row0=Sc - n_ctx, om_row0=3 * Sc if need_ctx else 0)
        xs = mm_residual(y, w_out[i], xs, g1, ctx_tiles=n_ctx // ROW_TILE)
        rows = xs.shape[1]

        h2, h2_tiles = norm_mod(xs, norm2_g[i], sc2, sh2, ctx_tiles=n_ctx // ROW_TILE, out_dtype=bf16, emit_tiled=True)
        table = pack_expert_tables(peer_u, peer_v, i)
        ff = peer_ffn(h2.reshape(B * rows, D), h2_tiles.reshape(B * rows, FEAT_ROWS, 128),
                      peer_wq, i, peer_k1[i], peer_k2[i], table).reshape(B * rows * FEAT_ROWS, 128)
        if need_ctx:
            xs, h1 = norm_mod(xs, norm1_g[i + 1], mods[i + 1][1], mods[i + 1][0], ctx_tiles=ctx_tiles, out_dtype=bf16,
                              res=ff, res_gate=g2, emit_x=True)
        else:
            zero = jnp.zeros((B, 2, 1, D), f32)
            out, = norm_mod(xs, final_g, zero, zero, ctx_tiles=0, out_dtype=f32, res=ff, res_gate=g2)
    return out
```

```python
import functools

import jax
import jax.numpy as jnp
from jax import lax
from jax.experimental import pallas as pl
from jax.experimental.pallas import tpu as pltpu

f32 = jnp.float32
bf16 = jnp.bfloat16

D_MODEL = 2048
GRID_W = 64
ROPE_THETA = 10000.0
NORM_EPS = 1e-6
LOG2E = 1.4426950408889634

GQA_HEADS = 8
GQA_KV_HEADS = 2
HEAD_DIM = 128
GLA_HEADS = 4
GLA_DK = 128
GLA_DV = 256
GLA_GATE_RANK = 16
GLA_TAU = 16.0
GLA_CHUNK = 64
GLA_SUB = 8
GLA_LR_COL = 64
MLA_HEADS = 8
MLA_Q_RANK = 512
MLA_KV_RANK = 512
MLA_NOPE = 128
MLA_ROPE = 64
MLA_V = 128
MLA_QK_PAD = 256
GQA_W = GQA_HEADS * HEAD_DIM
GLA_W = GLA_HEADS * GLA_DV
MLA_W = MLA_HEADS * MLA_V
PEER_HEADS = 8
PEER_NKEYS = 128
PEER_QDIM = 256
PEER_HALF = PEER_QDIM // 2
PEER_TOPK = 16
PEER_SEL = PEER_HEADS * PEER_TOPK
ROUTE_HEADS_PER_STEP = 4

VMEM_LIMIT = 48 * 1024 * 1024
ROW_TILE = 256


def _params(*sem):
    return pltpu.CompilerParams(dimension_semantics=sem, vmem_limit_bytes=VMEM_LIMIT)


def _mm_kernel(a_ref, b_ref, o_ref, bq_ref):
    @pl.when(pl.program_id(1) == 0)
    def _():
        bq_ref[...] = b_ref[...].astype(bf16)

    o_ref[...] = jnp.dot(a_ref[...].astype(bf16), bq_ref[...],
                         preferred_element_type=f32).astype(o_ref.dtype)


def mm(a, b, *, tm=512, tn=None, out_dtype=f32, layer=None):
    M, K = a.shape
    N = b.shape[-1]
    b_spec = (pl.BlockSpec((K, tn or N), lambda j, i: (0, j)) if layer is None
              else pl.BlockSpec((None, K, tn or N), lambda j, i: (layer, 0, j)))
    tn = N if tn is None else tn
    tm = min(tm, M)
    assert M % tm == 0 and N % tn == 0, (M, N, tm, tn)
    return pl.pallas_call(
        _mm_kernel,
        grid=(N // tn, M // tm),
        in_specs=[pl.BlockSpec((tm, K), lambda j, i: (i, 0)), b_spec],
        out_specs=pl.BlockSpec((tm, tn), lambda j, i: (i, j)),
        out_shape=jax.ShapeDtypeStruct((M, N), out_dtype),
        scratch_shapes=[pltpu.VMEM((K, tn), bf16)],
        compiler_params=_params("arbitrary", "arbitrary"),
        name="mm",
    )(a, b)


def _mm_rms_kernel(a_ref, g_ref, b_ref, o_ref, bq_ref):
    @pl.when(pl.program_id(1) == 0)
    def _():
        bq_ref[...] = b_ref[...].astype(bf16)

    a = a_ref[...]
    a = a * lax.rsqrt(jnp.mean(a * a, axis=-1, keepdims=True) + NORM_EPS) * g_ref[...]
    o_ref[...] = jnp.dot(a.astype(bf16), bq_ref[...], preferred_element_type=f32).astype(o_ref.dtype)


def mm_rms(a, a_col, g, b, *, tm=512, tn=None, out_dtype=f32):
    M = a.shape[0]
    K, N = b.shape
    tn = N if tn is None else tn
    assert M % tm == 0 and N % tn == 0
    return pl.pallas_call(
        _mm_rms_kernel,
        grid=(N // tn, M // tm),
        in_specs=[pl.BlockSpec((tm, K), lambda j, i: (i, a_col)),
                  pl.BlockSpec((1, K), lambda j, i: (0, 0)),
                  pl.BlockSpec((K, tn), lambda j, i: (0, j))],
        out_specs=pl.BlockSpec((tm, tn), lambda j, i: (i, j)),
        out_shape=jax.ShapeDtypeStruct((M, N), out_dtype),
        scratch_shapes=[pltpu.VMEM((K, tn), bf16)],
        compiler_params=_params("arbitrary", "arbitrary"),
        name="mm_rms",
    )(a, g.reshape(1, K), b)


FEAT_ROWS = D_MODEL // 128


def _norm_kernel(*refs, has_res, emit_x, emit_tiled):
    refs = list(refs)
    x_ref = refs.pop(0)
    x = x_ref[0]
    if has_res:
        ff_ref, gate_ref = refs.pop(0), refs.pop(0)
        ff = jnp.concatenate([ff_ref[pl.ds(r, ROW_TILE, stride=FEAT_ROWS), :] for r in range(FEAT_ROWS)], axis=1)
        x = x + gate_ref[0, 0] * ff
    g_ref, sc_ref, sh_ref = refs.pop(0), refs.pop(0), refs.pop(0)
    if emit_x:
        refs.pop(0)[0] = x
    y = x * lax.rsqrt(jnp.mean(x * x, axis=-1, keepdims=True) + NORM_EPS) * g_ref[...]
    h = y * (1.0 + sc_ref[0, 0]) + sh_ref[0, 0]
    h_ref = refs.pop(0)
    h_ref[0] = h.astype(h_ref.dtype)
    if emit_tiled:
        ht_ref = refs.pop(0)
        for r in range(FEAT_ROWS):
            ht_ref[pl.ds(r, ROW_TILE, stride=FEAT_ROWS), :] = h[:, r * 128:(r + 1) * 128]


def norm_mod(x, g, sc, sh, *, ctx_tiles, out_dtype, res=None, res_gate=None, emit_x=False, emit_tiled=False):
    B, R, D = x.shape
    nr = R // ROW_TILE
    kind = lambda r: jnp.where(r < ctx_tiles, 0, 1)
    row_spec = pl.BlockSpec((1, ROW_TILE, D), lambda b, r: (b, r, 0))
    mod_spec = pl.BlockSpec((1, 1, 1, D), lambda b, r: (b, kind(r), 0, 0))
    tile_spec = pl.BlockSpec((ROW_TILE * FEAT_ROWS, 128), lambda b, r: (b * nr + r, 0))
    args, in_specs = [x], [row_spec]
    if res is not None:
        args += [res, res_gate]
        in_specs += [tile_spec, mod_spec]
    args += [g.reshape(1, D), sc, sh]
    in_specs += [pl.BlockSpec((1, D), lambda b, r: (0, 0)), mod_spec, mod_spec]
    out_specs, out_shape = [], []
    if emit_x:
        out_specs.append(row_spec)
        out_shape.append(jax.ShapeDtypeStruct((B, R, D), f32))
    out_specs.append(row_spec)
    out_shape.append(jax.ShapeDtypeStruct((B, R, D), out_dtype))
    if emit_tiled:
        out_specs.append(tile_spec)
        out_shape.append(jax.ShapeDtypeStruct((B * R * FEAT_ROWS, 128), f32))
    return pl.pallas_call(
        functools.partial(_norm_kernel, has_res=res is not None, emit_x=emit_x, emit_tiled=emit_tiled),
        grid=(B, nr),
        in_specs=in_specs,
        out_specs=out_specs,
        out_shape=out_shape,
        compiler_params=_params("arbitrary", "arbitrary"),
        name="norm_mod",
    )(*args)


MERGE_TN = 1024


def _merge_kernel(oa_ref, of_ref, ob_ref, lg_ref, om_ref, ga_ref, gl_ref, gm_ref, wa_ref, wl_ref, wm_ref, gn_ref,
                  o_ref):
    o = of_ref[0] + ob_ref[0]
    lg = lg_ref[0]
    heads = []
    for h in range(GLA_HEADS):
        oh = o[:, h * GLA_DV:(h + 1) * GLA_DV]
        heads.append(oh * lax.rsqrt(jnp.mean(oh * oh, axis=-1, keepdims=True) + NORM_EPS) * gn_ref[...])
    o_l = jnp.concatenate(heads, axis=1) * (lg * jax.nn.sigmoid(lg))
    dot = lambda a, w: jnp.dot(a.astype(bf16), w[...], preferred_element_type=f32)
    sig = lambda r: jax.nn.sigmoid(r[0].astype(f32))
    y = (sig(ga_ref) * dot(oa_ref[0], wa_ref) + sig(gl_ref) * dot(o_l, wl_ref) + sig(gm_ref) * dot(om_ref[0], wm_ref))
    o_ref[0] = y.astype(o_ref.dtype)


def merge_branches(o_a, o_f, o_b, p_l, o_m, p_g, wa, wl, wm, gn, *, row0, om_row0=0):
    B, rows, _ = o_a.shape
    D = wa.shape[1]
    tm, tn = ROW_TILE, MERGE_TN
    nb = D // tn
    r0, rm = row0 // tm, om_row0 // tm
    own = lambda w, off=0: pl.BlockSpec((1, tm, w), lambda j, b, i: (b, i + off, 0))
    full = lambda w, cb: pl.BlockSpec((1, tm, w), lambda j, b, i: (b, i + r0, cb))
    gate = lambda k: pl.BlockSpec((1, tm, tn), lambda j, b, i: (b, i + r0, k * nb + j))
    wspec = pl.BlockSpec((GQA_W, tn), lambda j, b, i: (0, j))
    return pl.pallas_call(
        _merge_kernel,
        grid=(nb, B, rows // tm),
        in_specs=[own(GQA_W), full(GLA_W, 0), full(GLA_W, 0), full(GLA_W, 2), own(MLA_W, rm),
                  gate(0), gate(1), gate(2), wspec, wspec, wspec,
                  pl.BlockSpec((1, GLA_DV), lambda j, b, i: (0, 0))],
        out_specs=pl.BlockSpec((1, tm, tn), lambda j, b, i: (b, i, j)),
        out_shape=jax.ShapeDtypeStruct((B, rows, D), bf16),
        compiler_params=_params("arbitrary", "arbitrary", "arbitrary"),
        name="merge_branches",
    )(o_a, o_f, o_b, p_l, o_m, p_g, p_g, p_g, wa.astype(bf16), wl.astype(bf16), wm.astype(bf16), gn.reshape(1, GLA_DV))


def _mm_res_kernel(y_ref, w_ref, x_ref, g_ref, o_ref, wq_ref):
    @pl.when((pl.program_id(1) == 0) & (pl.program_id(2) == 0))
    def _():
        wq_ref[...] = w_ref[...].astype(bf16)

    o_ref[0] = x_ref[0] + g_ref[0, 0] * jnp.dot(y_ref[0], wq_ref[...], preferred_element_type=f32)


def mm_residual(y, w, x, gate, *, ctx_tiles, tn=1024):
    B, rows, K = y.shape
    N = w.shape[1]
    tm = ROW_TILE
    kind = lambda i: jnp.where(i < ctx_tiles, 0, 1)
    return pl.pallas_call(
        _mm_res_kernel,
        grid=(N // tn, B, rows // tm),
        in_specs=[pl.BlockSpec((1, tm, K), lambda j, b, i: (b, i, 0)),
                  pl.BlockSpec((K, tn), lambda j, b, i: (0, j)),
                  pl.BlockSpec((1, tm, tn), lambda j, b, i: (b, i, j)),
                  pl.BlockSpec((1, 1, 1, tn), lambda j, b, i: (b, kind(i), 0, j))],
        out_specs=pl.BlockSpec((1, tm, tn), lambda j, b, i: (b, i, j)),
        out_shape=jax.ShapeDtypeStruct((B, rows, N), f32),
        scratch_shapes=[pltpu.VMEM((K, tn), bf16)],
        compiler_params=_params("arbitrary", "arbitrary", "arbitrary"),
        name="mm_residual",
    )(y, w, x, gate)


def _rope_tile(y, tabs, half):
    c, s_lo, s_hi = tabs
    return y * c[...] + pltpu.roll(y, 128 - half, axis=1) * s_lo[...] + pltpu.roll(y, half, axis=1) * s_hi[...]


def _head_norm(x, g_ref):
    return x * lax.rsqrt(jnp.mean(x * x, axis=-1, keepdims=True) + NORM_EPS) * g_ref[...]


def _ones_column_tile(rows):
    lane = lax.broadcasted_iota(jnp.int32, (rows, 128), 1)
    return jnp.where(lane == 0, 1.0, 0.0).astype(bf16)


def _gqa_prep_kernel(p_ref, gq_ref, gk_ref, c_ref, s_lo_ref, s_hi_ref, q_ref, k_ref, v_ref):
    tabs = (c_ref, s_lo_ref, s_hi_ref)
    hd = HEAD_DIM
    q_scale = HEAD_DIM ** -0.5 * LOG2E
    for h in range(GQA_HEADS):
        x = p_ref[0, :, h * hd:(h + 1) * hd]
        q_ref[0, :, h * hd:(h + 1) * hd] = (_rope_tile(_head_norm(x, gq_ref), tabs, hd // 2) * q_scale).astype(bf16)
    k0 = GQA_HEADS * hd
    v0 = k0 + GQA_KV_HEADS * hd
    ones = _ones_column_tile(p_ref.shape[1])
    for h in range(GQA_KV_HEADS):
        x = p_ref[0, :, k0 + h * hd:k0 + (h + 1) * hd]
        k_ref[0, :, h * hd:(h + 1) * hd] = _rope_tile(_head_norm(x, gk_ref), tabs, hd // 2).astype(bf16)
        v_ref[0, :, 2 * h * hd:(2 * h + 1) * hd] = p_ref[0, :, v0 + h * hd:v0 + (h + 1) * hd].astype(bf16)
        v_ref[0, :, (2 * h + 1) * hd:(2 * h + 2) * hd] = ones


def gqa_prep(p_a, gq, gk, tabs):
    B, R, W = p_a.shape
    tm = ROW_TILE
    tab_spec = pl.BlockSpec((tm, 128), lambda b, r: (r, 0))
    gain_spec = pl.BlockSpec((1, HEAD_DIM), lambda b, r: (0, 0))
    row = lambda w: pl.BlockSpec((1, tm, w), lambda b, r: (b, r, 0))
    widths = (GQA_W, GQA_KV_HEADS * HEAD_DIM, GQA_KV_HEADS * 2 * HEAD_DIM)
    return pl.pallas_call(
        _gqa_prep_kernel,
        grid=(B, R // tm),
        in_specs=[row(W), gain_spec, gain_spec, tab_spec, tab_spec, tab_spec],
        out_specs=[row(w) for w in widths],
        out_shape=[jax.ShapeDtypeStruct((B, R, w), bf16) for w in widths],
        compiler_params=_params("arbitrary", "arbitrary"),
        name="gqa_prep",
    )(p_a, gq.reshape(1, HEAD_DIM), gk.reshape(1, HEAD_DIM), *tabs)


def _mla_prep_kernel(mq_ref, mkv_ref, ps_ref, c_ref, s_lo_ref, s_hi_ref, q_ref, k_ref, v_ref):
    tabs = (c_ref, s_lo_ref, s_hi_ref)
    half = MLA_ROPE // 2
    q_scale = (MLA_NOPE + MLA_ROPE) ** -0.5 * LOG2E
    k_rope = _rope_tile(ps_ref[0], tabs, half).astype(bf16)
    ones = _ones_column_tile(mq_ref.shape[1])
    P = MLA_QK_PAD
    for h in range(MLA_HEADS):
        lo, mid, hi = h * P, h * P + 128, (h + 1) * P
        q_ref[0, :, lo:mid] = (mq_ref[0, :, lo:mid] * q_scale).astype(bf16)
        q_ref[0, :, mid:hi] = (_rope_tile(mq_ref[0, :, mid:hi], tabs, half) * q_scale).astype(bf16)
        k_ref[0, :, lo:mid] = mkv_ref[0, :, lo:mid].astype(bf16)
        k_ref[0, :, mid:hi] = k_rope
        v_ref[0, :, lo:mid] = mkv_ref[0, :, mid:hi].astype(bf16)
        v_ref[0, :, mid:hi] = ones


def mla_prep(mq, mkv, p_s, tabs):
    B, R, W = mq.shape
    tm = ROW_TILE
    tab_spec = pl.BlockSpec((tm, 128), lambda b, r: (r, 0))
    row = lambda w: pl.BlockSpec((1, tm, w), lambda b, r: (b, r, 0))
    return pl.pallas_call(
        _mla_prep_kernel,
        grid=(B, R // tm),
        in_specs=[row(W), row(W), row(128), tab_spec, tab_spec, tab_spec],
        out_specs=[row(W)] * 3,
        out_shape=[jax.ShapeDtypeStruct((B, R, W), bf16)] * 3,
        compiler_params=_params("arbitrary", "arbitrary"),
        name="mla_prep",
    )(mq, mkv, p_s, *tabs)


ATT_ROWS = 256


def _attn_kernel(q_ref, k_ref, v_ref, o_ref, m_ref, acc_ref, *, tk, nk, group, splits, dq, dv):
    R = ATT_ROWS
    chains = [(g, r) for g in range(group) for r in range(splits)]
    m_ref[...] = jnp.full(m_ref.shape, -jnp.inf, f32)
    acc_ref[...] = jnp.zeros(acc_ref.shape, f32)

    def body(c, carry):
        off = pl.multiple_of(c * tk, tk)
        kc = k_ref[0, pl.ds(off, tk), :]
        vc = v_ref[0, pl.ds(off, tk), :]
        for n, (g, r) in enumerate(chains):
            q = q_ref[0, r * R:(r + 1) * R, g * dq:(g + 1) * dq]
            s = lax.dot_general(q, kc, (((1,), (1,)), ((), ())), preferred_element_type=f32)
            m_old = m_ref[n]
            m_new = jnp.maximum(m_old, jnp.max(s, axis=1, keepdims=True))
            p = jnp.exp2(s - m_new)
            alpha = jnp.exp2(m_old - m_new)
            acc_ref[n] = alpha * acc_ref[n] + jnp.dot(p.astype(bf16), vc, preferred_element_type=f32)
            m_ref[n] = m_new
        return carry

    lax.fori_loop(0, nk, body, 0)
    for n, (g, r) in enumerate(chains):
        acc = acc_ref[n]
        o_ref[0, r * R:(r + 1) * R, g * dv:(g + 1) * dv] = (acc[:, :dv] / acc[:, dv:dv + 1]).astype(o_ref.dtype)


def _attn_fill_kernel(kern, q_ref, k_ref, v_ref, fill_ref, o_ref, *scratch):
    del fill_ref
    kern(q_ref, k_ref, v_ref, o_ref, *scratch)


def attention(q, k, v, *, heads, kv_heads, dq, dv, tq, tk, q_start=0, q_rows=None, out_rows=None, out_start=0,
              fill=None):
    B, Sq, _ = q.shape
    Sk = k.shape[1]
    q_rows = Sq - q_start if q_rows is None else q_rows
    out_rows = q_rows if out_rows is None else out_rows
    group = heads // kv_heads
    splits = tq // ATT_ROWS
    assert q_rows % tq == 0 and q_start % tq == 0 and out_start % tq == 0 and Sk % tk == 0 and tq % ATT_ROWS == 0
    first, out_first = q_start // tq, out_start // tq
    chains = group * splits
    in_specs = [pl.BlockSpec((1, tq, group * dq), lambda b, h, i: (b, i + first, h)),
                pl.BlockSpec((1, Sk, dq), lambda b, h, i: (b, 0, h)),
                pl.BlockSpec((1, Sk, 2 * dv), lambda b, h, i: (b, 0, h))]
    args = [q, k, v]
    kern = functools.partial(_attn_kernel, tk=tk, nk=Sk // tk, group=group, splits=splits, dq=dq, dv=dv)
    if fill is not None:
        in_specs.append(pl.BlockSpec(memory_space=pl.ANY))
        args.append(fill)
        kern = functools.partial(_attn_fill_kernel, kern)
    return pl.pallas_call(
        kern,
        grid=(B, kv_heads, q_rows // tq),
        in_specs=in_specs,
        out_specs=pl.BlockSpec((1, tq, group * dv), lambda b, h, i: (b, i + out_first, h)),
        out_shape=jax.ShapeDtypeStruct((B, out_rows, heads * dv), f32),
        scratch_shapes=[pltpu.VMEM((chains, ATT_ROWS, 1), f32),
                        pltpu.VMEM((chains, ATT_ROWS, 2 * dv), f32)],
        input_output_aliases={3: 0} if fill is not None else {},
        compiler_params=_params("arbitrary", "arbitrary", "arbitrary"),
        name="attention",
    )(*args)


def _exact_dot_01(t01, x):
    h1 = x.astype(bf16)
    r1 = x - h1.astype(f32)
    h2 = r1.astype(bf16)
    h3 = (r1 - h2.astype(f32)).astype(bf16)
    d = lambda h: jnp.dot(t01, h, preferred_element_type=f32)
    return d(h1) + d(h2) + d(h3)


def _gla_kernel(q_ref, k_ref, v_ref, lr_ref, wa_ref, ba_ref, o_ref, state_ref, b_ref, *, rev):
    c = pl.program_id(1)
    C, SB = GLA_CHUNK, GLA_SUB
    nt = (((1,), (1,)), ((), ()))

    @pl.when(c == 0)
    def _():
        state_ref[...] = jnp.zeros_like(state_ref)

    row = lax.broadcasted_iota(jnp.int32, (C, C), 0)
    col = lax.broadcasted_iota(jnp.int32, (C, C), 1)
    tri = (row <= col) if rev else (row >= col)
    lr0 = GLA_LR_COL + int(rev) * GLA_GATE_RANK
    lr = lr_ref[0][:, lr0:lr0 + GLA_GATE_RANK]
    pre = jnp.dot(lr.astype(bf16), wa_ref[...], preferred_element_type=f32) + ba_ref[...]
    log_a = (jnp.minimum(pre, 0.0) - jnp.log1p(jnp.exp(-jnp.abs(pre)))) * (1.0 / GLA_TAU)
    b_ref[...] = _exact_dot_01(tri.astype(bf16), log_a)
    end = 0 if rev else C - 1
    krow = lax.broadcasted_iota(jnp.int32, (C, GLA_DK), 0)
    lane = lax.broadcasted_iota(jnp.int32, (SB, C), 1)
    trow = lax.broadcasted_iota(jnp.int32, (SB, C), 0)

    for h in range(GLA_HEADS):
        ks = slice(h * GLA_DK, (h + 1) * GLA_DK)
        vs = slice(h * GLA_DV, (h + 1) * GLA_DV)
        qh = q_ref[0, :, ks] * (GLA_DK ** -0.5)
        kh = k_ref[0, :, ks]
        vh = v_ref[0, :, vs].astype(bf16)
        bh = b_ref[:, ks]
        b_end = b_ref[end:end + 1, ks]
        st = state_ref[h]

        o_inter = lax.dot_general((qh * jnp.exp(bh)).astype(bf16), st.astype(bf16), nt,
                                  preferred_element_type=f32)

        blocks = []
        for i in range(C // SB):
            r0 = i * SB
            qi, bi = qh[r0:r0 + SB], bh[r0:r0 + SB]
            has_earlier = i < C // SB - 1 if rev else i > 0
            if has_earlier:
                first = r0 + SB - 1 if rev else r0
                b_first = b_ref[first:first + 1, ks]
                earlier = (krow >= r0 + SB) if rev else (krow < r0)
                k_dec = jnp.where(earlier, kh * jnp.exp(jnp.minimum(b_first - bh, 0.0)), 0.0)
                q_dec = qi * jnp.exp(jnp.minimum(bi - b_first, 0.0))
                blk = lax.dot_general(q_dec.astype(bf16), k_dec.astype(bf16), nt, preferred_element_type=f32)
            else:
                blk = jnp.zeros((SB, C), f32)
            diag = jnp.zeros((SB, C), f32)
            for u in range(SB):
                s = r0 + u
                x = qi * jnp.exp(jnp.minimum(bi - b_ref[s:s + 1, ks], 0.0)) * k_ref[0, s:s + 1, ks]
                diag = jnp.where(lane == s, jnp.sum(x, axis=1, keepdims=True), diag)
            causal = (lane >= trow + r0) if rev else (lane <= trow + r0)
            blocks.append(blk + jnp.where(causal, diag, 0.0))
        scores = jnp.concatenate(blocks, axis=0)
        o_intra = jnp.dot(scores.astype(bf16), vh, preferred_element_type=f32)
        o_ref[0, :, vs] = o_inter + o_intra

        k_end = (kh * jnp.exp(b_end - bh)).astype(bf16)
        state_ref[h] = st * jnp.exp(b_end) + lax.dot_general(
            vh, k_end, (((0,), (0,)), ((), ())), preferred_element_type=f32)


def gla_scans(p_l, p_s, wa, ba, *, ctx_len):
    B, R, _ = p_l.shape
    C = GLA_CHUNK
    n = R // C
    cc = ctx_len // C
    wide = GLA_HEADS * GLA_DK
    assert GLA_W == 2 * wide

    def scan(rev):
        chunk = (lambda c: jnp.where(c < cc, cc - 1 - c, n - 1 + cc - c)) if rev else (lambda c: c)
        d = int(rev)
        return pl.pallas_call(
            functools.partial(_gla_kernel, rev=rev),
            grid=(B, n),
            in_specs=[pl.BlockSpec((1, C, wide), lambda b, c: (b, chunk(c), 0)),
                      pl.BlockSpec((1, C, wide), lambda b, c: (b, chunk(c), 1)),
                      pl.BlockSpec((1, C, GLA_W), lambda b, c: (b, chunk(c), 1)),
                      pl.BlockSpec((1, C, 128), lambda b, c: (b, chunk(c), 0)),
                      pl.BlockSpec((GLA_GATE_RANK, wide), lambda b, c: (0, 0)),
                      pl.BlockSpec((1, wide), lambda b, c: (0, 0))],
            out_specs=pl.BlockSpec((1, C, GLA_W), lambda b, c: (b, chunk(c), 0)),
            out_shape=jax.ShapeDtypeStruct((B, R, GLA_W), f32),
            scratch_shapes=[pltpu.VMEM((GLA_HEADS, GLA_DV, GLA_DK), f32),
                            pltpu.VMEM((C, wide), f32)],
            compiler_params=_params("arbitrary", "arbitrary"),
            name="gla_scan_rev" if rev else "gla_scan_fwd",
        )(p_l, p_l, p_l, p_s, wa[d], ba[d])

    return scan(False), scan(True)


def _top16(s, ids):
    big = jnp.float32(1e9)
    vals, picks = [], []
    for _ in range(PEER_TOPK):
        m = jnp.max(s, axis=0, keepdims=True)
        pick = jnp.min(jnp.where(s == m, ids, big), axis=0, keepdims=True)
        vals.append(m)
        picks.append(pick)
        s = jnp.where(ids == pick, -jnp.inf, s)
    return jnp.concatenate(vals, axis=0), jnp.concatenate(picks, axis=0)


def _route_head(q, k1, k2):
    nt = (((1,), (1,)), ((), ()))
    s1 = lax.dot_general(k1, q[:, :PEER_HALF], nt, preferred_element_type=f32)
    s2 = lax.dot_general(k2, q[:, PEER_HALF:], nt, preferred_element_type=f32)
    key_id = lax.broadcasted_iota(jnp.int32, s1.shape, 0).astype(f32)
    v1, i1 = _top16(s1, key_id)
    v2, i2 = _top16(s2, key_id)
    half = PEER_TOPK // 2
    sub = lax.broadcasted_iota(jnp.int32, (PEER_TOPK, v1.shape[1]), 0).astype(f32)
    nb = lambda a: PEER_TOPK if a == 0 else half
    cand = jnp.concatenate([v1[a:a + 1] + v2[:nb(a)] for a in range(half)] + [v1[half:] + v2[0:1]], axis=0)
    cand_e = jnp.concatenate([i1[a:a + 1] * PEER_NKEYS + i2[:nb(a)] for a in range(half)]
                             + [i1[half:] * PEER_NKEYS + i2[0:1]], axis=0)
    pos = jnp.concatenate([a * PEER_TOPK + sub[:nb(a)] for a in range(half)]
                          + [(sub[:half] + half) * PEER_TOPK], axis=0)
    best, bpos = _top16(cand, pos)
    experts = [jnp.max(jnp.where(pos == bpos[r:r + 1], cand_e, -1.0), axis=0, keepdims=True)
               for r in range(PEER_TOPK)]
    e = jnp.exp(best - best[0:1])
    return jnp.concatenate(experts, axis=0).astype(jnp.int32), e / jnp.sum(e, axis=0, keepdims=True)


def _router_kernel(q_ref, k1_ref, k2_ref, idx_ref, g_ref):
    for h in range(ROUTE_HEADS_PER_STEP):
        q = q_ref[:, h * PEER_QDIM:(h + 1) * PEER_QDIM].astype(bf16)
        idx_ref[h], g_ref[h] = _route_head(q, k1_ref[h], k2_ref[h])


def peer_route(q, k1, k2):
    T = q.shape[0]
    tt = 128
    hs = ROUTE_HEADS_PER_STEP
    return pl.pallas_call(
        _router_kernel,
        grid=(T // tt, PEER_HEADS // hs),
        in_specs=[pl.BlockSpec((tt, hs * PEER_QDIM), lambda i, h: (i, h)),
                  pl.BlockSpec((hs, PEER_NKEYS, PEER_HALF), lambda i, h: (h, 0, 0)),
                  pl.BlockSpec((hs, PEER_NKEYS, PEER_HALF), lambda i, h: (h, 0, 0))],
        out_specs=[pl.BlockSpec((hs, PEER_TOPK, tt), lambda i, h: (h, 0, i)),
                   pl.BlockSpec((hs, PEER_TOPK, tt), lambda i, h: (h, 0, i))],
        out_shape=[jax.ShapeDtypeStruct((PEER_HEADS, PEER_TOPK, T), jnp.int32),
                   jax.ShapeDtypeStruct((PEER_HEADS, PEER_TOPK, T), f32)],
        compiler_params=_params("arbitrary", "arbitrary"),
        name="peer_route",
    )(q, k1, k2)


PEER_TT = 128
PEER_AHEAD = 3
PEER_SLOTS = PEER_AHEAD + 1
PEER_NEXT = 8
assert PEER_NEXT >= PEER_AHEAD and ((PEER_TT + PEER_NEXT) * PEER_SEL) % 1024 == 0
assert PEER_TT % PEER_SLOTS == 0


def _unpack_words(w):
    lo = pltpu.bitcast(w << 16, f32)
    hi = pltpu.bitcast(w & jnp.uint32(0xFFFF0000), f32)
    return lo, hi


def _sum_sublanes_8(parts):
    sub = lax.broadcasted_iota(jnp.int32, (8, 128), 0)
    dist = 4
    while len(parts) > 1:
        nxt = []
        half = len(parts) // 2
        for a in range(half):
            lo_rows = parts[a] + pltpu.roll(parts[a], 8 - dist, axis=0)
            hi_rows = parts[a + half] + pltpu.roll(parts[a + half], dist, axis=0)
            nxt.append(jnp.where((sub & dist) == 0, lo_rows, hi_rows))
        parts = nxt
        dist //= 2
    return parts[0]


def _peer_kernel(idx_ref, tab_ref, h_ref, g_ref, o_ref, *scratch):
    bufs, (sem, part_ref, w_ref) = scratch[:PEER_SLOTS], scratch[PEER_SLOTS:]
    NS = PEER_SEL
    step = pl.program_id(0)

    def issue(t, slot):
        for j in range(NS):
            e = idx_ref[t * NS + j]
            pltpu.make_async_copy(tab_ref.at[e], bufs[slot].at[:, :, j, :], sem.at[slot]).start(priority=j % 2)

    def wait(slot):
        pltpu.make_async_copy(bufs[slot], bufs[slot], sem.at[slot]).wait()

    @pl.when(step == 0)
    def _():
        for t in range(PEER_AHEAD):
            issue(t, t)

    lane = lax.broadcasted_iota(jnp.int32, (NS, PEER_TT), 1)

    def compute(t, slot):
        buf = bufs[slot]
        hrow = h_ref[t]
        hb = [jnp.broadcast_to(hrow[r:r + 1], (8, 128)) for r in range(16)]
        for jg in range(NS // 8):
            js = slice(jg * 8, (jg + 1) * 8)
            acc = None
            for s in range(8):
                lo, hi = _unpack_words(buf[0, s, js, :])
                term = lo * hb[2 * s] + hi * hb[2 * s + 1]
                acc = term if acc is None else acc + term
            part_ref[js, :] = acc
        act = jnp.sum(part_ref[...], axis=1, keepdims=True)
        gate = jnp.sum(jnp.where(lane == t, g_ref[...], 0.0), axis=1, keepdims=True)
        w = gate * (0.5 * act * (1.0 + lax.erf(act * 0.7071067811865476)))
        w_ref[...] = jnp.broadcast_to(w, (NS, 128))
        rows = []
        for s in range(8):
            acc_lo = acc_hi = None
            for jg in range(NS // 8):
                js = slice(jg * 8, (jg + 1) * 8)
                lo, hi = _unpack_words(buf[1, s, js, :])
                wv = w_ref[js, :]
                acc_lo = wv * lo if acc_lo is None else acc_lo + wv * lo
                acc_hi = wv * hi if acc_hi is None else acc_hi + wv * hi
            rows += [acc_lo, acc_hi]
        o_ref[t] = jnp.concatenate([_sum_sublanes_8(rows[0:8]), _sum_sublanes_8(rows[8:16])], axis=0)

    def group(tg, carry):
        for u in range(PEER_SLOTS):
            t = tg * PEER_SLOTS + u
            wait(u)
            issue(t + PEER_AHEAD, (u + PEER_AHEAD) % PEER_SLOTS)
            compute(t, u)
        return carry

    lax.fori_loop(0, PEER_TT // PEER_SLOTS, group, 0)

    @pl.when(step == pl.num_programs(0) - 1)
    def _():
        for t in range(PEER_AHEAD):
            wait(t)


def peer_experts(idx, table, h, gates):
    T = h.shape[0]
    tt = PEER_TT
    tiles = idx.reshape(T // tt, tt, PEER_SEL)
    idx_ext = jnp.concatenate([tiles, jnp.roll(tiles, -1, axis=0)[:, :PEER_NEXT]], axis=1).reshape(-1)
    return pl.pallas_call(
        _peer_kernel,
        grid=(T // tt,),
        in_specs=[pl.BlockSpec(((tt + PEER_NEXT) * PEER_SEL,), lambda i: (i,), memory_space=pltpu.SMEM),
                  pl.BlockSpec(memory_space=pl.ANY),
                  pl.BlockSpec((tt, 16, 128), lambda i: (i, 0, 0)),
                  pl.BlockSpec((PEER_SEL, tt), lambda i: (0, i))],
        out_specs=pl.BlockSpec((tt, 16, 128), lambda i: (i, 0, 0)),
        out_shape=jax.ShapeDtypeStruct((T, 16, 128), f32),
        scratch_shapes=[pltpu.VMEM((2, 8, PEER_SEL, 128), jnp.uint32)] * PEER_SLOTS + [
                        pltpu.SemaphoreType.DMA((PEER_SLOTS,)),
                        pltpu.VMEM((PEER_SEL, 128), f32),
                        pltpu.VMEM((PEER_SEL, 128), f32)],
        compiler_params=_params("arbitrary"),
        name="peer_experts",
    )(idx_ext, table, h, gates)


def _round_bf16_bits(x):
    bits = pltpu.bitcast(x, jnp.uint32)
    return bits + jnp.uint32(0x7FFF) + ((bits >> 16) & jnp.uint32(1))


def _pack_kernel(u_ref, v_ref, o_ref):
    te = u_ref.shape[0]
    for m, ref in enumerate((u_ref, v_ref)):
        for s in range(8):
            lo = _round_bf16_bits(ref[:, (2 * s) * 128:(2 * s + 1) * 128])
            hi = _round_bf16_bits(ref[:, (2 * s + 1) * 128:(2 * s + 2) * 128])
            o_ref[pl.ds(m * 8 + s, te, stride=16), :] = (lo >> 16) | (hi & jnp.uint32(0xFFFF0000))


def pack_expert_tables(u, v, layer):
    _, E, D = u.shape
    te = 512
    out = pl.pallas_call(
        _pack_kernel,
        grid=(E // te,),
        in_specs=[pl.BlockSpec((None, te, D), lambda i: (layer, i, 0))] * 2,
        out_specs=pl.BlockSpec((te * 16, 128), lambda i: (i, 0)),
        out_shape=jax.ShapeDtypeStruct((E * 16, 128), jnp.uint32),
        compiler_params=_params("arbitrary"),
        name="pack_expert_tables",
    )(u, v)
    return out.reshape(E, 2, 8, 128)


def peer_ffn(h2, h2_tiles, wq, layer, k1, k2, table):
    T = h2.shape[0]
    q = mm(h2, wq, tn=1024, layer=layer)
    idx, gates = peer_route(q, k1.astype(bf16), k2.astype(bf16))
    idx_tok = idx.reshape(PEER_SEL, T).T
    return peer_experts(idx_tok, table, h2_tiles, gates.reshape(PEER_SEL, T))


def _rope_tables(n_ctx, seq, dim):
    quarter = dim // 4
    t = jnp.arange(seq, dtype=jnp.int32)
    inv_freq = ROPE_THETA ** (-jnp.arange(quarter, dtype=f32) / quarter)
    ang = jnp.concatenate([(t // GRID_W)[:, None].astype(f32) * inv_freq,
                           (t % GRID_W)[:, None].astype(f32) * inv_freq], axis=-1)
    cos = jnp.concatenate([jnp.ones((n_ctx, dim // 2), f32), jnp.cos(ang)], axis=0)
    sin = jnp.concatenate([jnp.zeros((n_ctx, dim // 2), f32), jnp.sin(ang)], axis=0)
    return cos, sin


def _rope_tile_tables(cos, sin):
    R, half = cos.shape
    z = jnp.zeros((R, 128 - 2 * half), f32)
    zh = jnp.zeros((R, half), f32)
    return (jnp.concatenate([cos, cos, z], axis=1),
            jnp.concatenate([-sin, zh, z], axis=1),
            jnp.concatenate([zh, sin, z], axis=1))


def _ada(vec, w, layer, b):
    n = vec.shape[0]
    a = jnp.zeros((8, vec.shape[1]), f32).at[:n].set(jax.nn.silu(vec))
    return mm(a, w, tm=8, tn=1024, layer=layer)[:n] + b


def kernel(x, c, ctx, c_ctx, w_mod, b_mod, norm1_g, w_in, gqa_qn_g, gqa_kn_g, gla_wa2_f, gla_ba_f, gla_wa2_b, gla_ba_b, gla_on_g, mla_qn_g, mla_wuq, mla_kvn_g, mla_wukv, w_br_gqa, w_br_gla, w_br_mla, w_out, norm2_g, peer_wq, peer_k1, peer_k2, peer_u, peer_v, final_g):
    B, S, D = x.shape
    Sc = ctx.shape[1]
    depth = w_in.shape[0]
    R = Sc + S
    ctx_tiles = Sc // ROW_TILE
    tabs_a = _rope_tile_tables(*_rope_tables(Sc, S, HEAD_DIM))
    tabs_m = _rope_tile_tables(*_rope_tables(Sc, S, MLA_ROPE))
    xs = jnp.concatenate([ctx, x], axis=1)

    mods = []
    for i in range(depth):
        m = _ada(jnp.concatenate([c_ctx[None, :], c], axis=0), w_mod, i, b_mod[i])
        m = jnp.stack([jnp.broadcast_to(m[0], (B, 6 * D)), m[1:]], axis=1)
        mods.append([t[:, :, None, :] for t in jnp.split(m, 6, axis=-1)])

    h1, = norm_mod(xs, norm1_g[0], mods[0][1], mods[0][0], ctx_tiles=ctx_tiles, out_dtype=bf16)
    for i in range(depth):
        need_ctx = i < depth - 1
        sh1, sc1, g1, sh2, sc2, g2 = mods[i]

        h1 = h1.reshape(B * R, D)
        w = w_in[i]
        p_a = mm(h1, w[:, 0:1536], tn=768).reshape(B, R, 1536)
        p_l = mm(h1, w[:, 1536:4608], tn=1024).reshape(B, R, 3072)
        p_m = mm(h1, w[:, 4640:5664], tn=1024).reshape(B, R, 1024)
        w_small = jnp.concatenate([w[:, 5664:5728], w[:, 4608:4640], jnp.zeros((D, 32), f32)], axis=1)
        p_s = mm(h1, w_small, tn=128).reshape(B, R, 128)
        p_g = mm(h1, w[:, 5728:11872], tn=1024, out_dtype=bf16).reshape(B, R, 3 * D)

        aq, ak, av = gqa_prep(p_a, gqa_qn_g[i], gqa_kn_g[i], tabs_a)
        gqa = functools.partial(attention, heads=GQA_HEADS, kv_heads=GQA_KV_HEADS, dq=HEAD_DIM, dv=HEAD_DIM)
        if need_ctx:
            o_a = gqa(aq, ak, av, tq=ATT_ROWS, tk=R // 2, q_start=Sc, out_rows=R, out_start=Sc,
                      fill=jnp.zeros((B, R, GQA_W), f32))
            o_a = gqa(aq, ak[:, :Sc], av[:, :Sc], tq=Sc, tk=Sc, q_rows=Sc, out_rows=R, fill=o_a)
        else:
            o_a = gqa(aq, ak, av, tq=ATT_ROWS, tk=R // 2, q_start=Sc)

        wuq = jnp.pad(mla_wuq[i].reshape(MLA_Q_RANK, MLA_HEADS, MLA_NOPE + MLA_ROPE),
                      ((0, 0), (0, 0), (0, MLA_QK_PAD - MLA_NOPE - MLA_ROPE))).reshape(MLA_Q_RANK, MLA_HEADS * MLA_QK_PAD)
        p_m2 = p_m.reshape(B * R, MLA_Q_RANK + MLA_KV_RANK)
        mq = mm_rms(p_m2, 0, mla_qn_g[i], wuq, tn=1024).reshape(B, R, MLA_HEADS * MLA_QK_PAD)
        mkv = mm_rms(p_m2, 1, mla_kvn_g[i], mla_wukv[i], tn=1024).reshape(B, R, MLA_HEADS * (MLA_NOPE + MLA_V))
        m_q, m_k, m_v = mla_prep(mq, mkv, p_s, tabs_m)
        mla = functools.partial(attention, heads=MLA_HEADS, kv_heads=MLA_HEADS, dq=MLA_QK_PAD, dv=MLA_V)
        if need_ctx:
            o_m = mla(m_q[:, Sc:], m_k, m_v, tq=4 * ATT_ROWS, tk=R, out_rows=R + 3 * Sc, out_start=4 * Sc,
                      fill=jnp.zeros((B, R + 3 * Sc, MLA_W), f32))
            o_m = mla(m_q, m_k[:, :Sc], m_v[:, :Sc], tq=Sc, tk=Sc, q_rows=Sc, out_rows=R + 3 * Sc, out_start=3 * Sc,
                      fill=o_m)
        else:
            o_m = mla(m_q[:, Sc:], m_k, m_v, tq=4 * ATT_ROWS, tk=R)

        wa = jnp.stack([gla_wa2_f[i], gla_wa2_b[i]], axis=0).astype(bf16)
        ba = jnp.stack([gla_ba_f[i], gla_ba_b[i]], axis=0)[:, None, :]
        o_f, o_b = gla_scans(p_l, p_s, wa, ba, ctx_len=Sc)

        n_ctx = Sc if need_ctx else 0
        if not need_ctx:
            xs = xs[:, Sc:]
        y = merge_branches(o_a, o_f, o_b, p_l, o_m, p_g, w_br_gqa[i], w_br_gla[i], w_br_mla[i], gla_on_g[i],
                           row0=Sc - n_ctx, om_row0=3 * Sc if need_ctx else 0)
        xs = mm_residual(y, w_out[i], xs, g1, ctx_tiles=n_ctx // ROW_TILE)
        rows = xs.shape[1]

        h2, h2_tiles = norm_mod(xs, norm2_g[i], sc2, sh2, ctx_tiles=n_ctx // ROW_TILE, out_dtype=bf16, emit_tiled=True)
        table = pack_expert_tables(peer_u, peer_v, i)
        ff = peer_ffn(h2.reshape(B * rows, D), h2_tiles.reshape(B * rows, FEAT_ROWS, 128),
                      peer_wq, i, peer_k1[i], peer_k2[i], table).reshape(B * rows * FEAT_ROWS, 128)
        if need_ctx:
            xs, h1 = norm_mod(xs, norm1_g[i + 1], mods[i + 1][1], mods[i + 1][0], ctx_tiles=ctx_tiles, out_dtype=bf16,
                              res=ff, res_gate=g2, emit_x=True)
        else:
            zero = jnp.zeros((B, 2, 1, D), f32)
            out, = norm_mod(xs, final_g, zero, zero, ctx_tiles=0, out_dtype=f32, res=ff, res_gate=g2)
    return out
```
